```python
import math
import jax, jax.numpy as jnp
from jax import lax
import numpy as np

D_MODEL = 4096
BATCH = 2
SEQ = 4096
DEPTH = 2

N_MIXERS = 2
N_ATTN_LAYERS = (DEPTH + 1) // 2
N_CONV_LAYERS = DEPTH // 2

HEAD_DIM = 128
N_HEADS = D_MODEL // HEAD_DIM
N_KV_HEADS = N_HEADS // 4
GROUP = N_HEADS // N_KV_HEADS
Q_WIDTH = N_HEADS * HEAD_DIM
KV_WIDTH = N_KV_HEADS * HEAD_DIM
WINDOW = 128
BLOCK = 128

CONV_WIDTH = 3
CONV_CH = D_MODEL

FFN_HIDDEN = int(math.ceil((8 * D_MODEL / 3) / 256) * 256)

ALPHA = (2.0 * DEPTH) ** 0.25
BETA = (8.0 * DEPTH) ** -0.25
LN_EPS = 1e-5

kernel_name = "hybrid_swa_sink_shortconv_deepnorm"


def layer_norm(x, g, b):
    xf = x.astype(jnp.float32)
    mu = jnp.mean(xf, axis=-1, keepdims=True)
    var = jnp.mean(jnp.square(xf - mu), axis=-1, keepdims=True)
    y = (xf - mu) * lax.rsqrt(var + LN_EPS)
    return (y * g.astype(jnp.float32) + b.astype(jnp.float32)).astype(x.dtype)


def _banded(t, n_blocks):
    b_, s_ = t.shape[0], t.shape[1]
    tp = jnp.pad(t, ((0, 0), (BLOCK, 0), (0, 0), (0, 0)))
    prev = tp[:, :s_].reshape(b_, n_blocks, BLOCK, N_KV_HEADS, HEAD_DIM)
    cur = t.reshape(b_, n_blocks, BLOCK, N_KV_HEADS, HEAD_DIM)
    return jnp.concatenate([prev, cur], axis=2)


def sliding_window_sink_attention(x, w_in, sinks, w_out):
    b_, s_, _ = x.shape
    nb = s_ // BLOCK
    qkv = x @ w_in
    q, k, v = jnp.split(qkv, [Q_WIDTH, Q_WIDTH + KV_WIDTH], axis=-1)
    q = q.reshape(b_, nb, BLOCK, N_KV_HEADS, GROUP, HEAD_DIM)
    kb = _banded(k.reshape(b_, s_, N_KV_HEADS, HEAD_DIM), nb)
    vb = _banded(v.reshape(b_, s_, N_KV_HEADS, HEAD_DIM), nb)

    scale = 1.0 / math.sqrt(HEAD_DIM)
    scores = jnp.einsum('bnqhgd,bnkhd->bnhgqk', q, kb).astype(jnp.float32) * scale
    qi = jnp.arange(BLOCK)[:, None]
    ki = jnp.arange(2 * BLOCK)[None, :]
    diff = qi + BLOCK - ki
    band = (diff >= 0) & (diff < WINDOW)
    key_pos = jnp.arange(nb)[:, None] * BLOCK + jnp.arange(2 * BLOCK)[None, :] - BLOCK
    mask = band[None, :, :] & (key_pos >= 0)[:, None, :]
    scores = jnp.where(mask[None, :, None, None, :, :], scores, -jnp.inf)

    sink = sinks.astype(jnp.float32).reshape(N_KV_HEADS, GROUP)[None, None, :, :, None, None]
    m = jnp.maximum(jnp.max(scores, axis=-1, keepdims=True), sink)
    p = jnp.exp(scores - m)
    denom = jnp.sum(p, axis=-1, keepdims=True) + jnp.exp(sink - m)
    probs = (p / denom).astype(vb.dtype)
    out = jnp.einsum('bnhgqk,bnkhd->bnqhgd', probs, vb).reshape(b_, s_, Q_WIDTH)
    return out @ w_out


def short_conv_mixer(x, w_in, conv_w, w_out):
    s_ = x.shape[1]
    b_gate, c_gate, h = jnp.split(x @ w_in, 3, axis=-1)
    u = c_gate * h
    up = jnp.pad(u, ((0, 0), (CONV_WIDTH - 1, 0), (0, 0)))
    conv = up[:, 0:s_] * conv_w[0]
    for j in range(1, CONV_WIDTH):
        conv = conv + up[:, j:j + s_] * conv_w[j]
    return (b_gate * conv) @ w_out


def swiglu_ffn(x, w_gate_up, w_down):
    gate, up = jnp.split(x @ w_gate_up, 2, axis=-1)
    return (jax.nn.silu(gate) * up) @ w_down


def setup_inputs(seed: int = 0) -> dict:
    key = jax.random.key(seed)
    ks = jax.random.split(key, 16)

    def normal(k, shape, scale):
        return jax.random.normal(k, shape, jnp.float32) * scale

    d = D_MODEL
    x = normal(ks[0], (BATCH, SEQ, d), 1.0)
    attn_w_in = normal(ks[1], (N_ATTN_LAYERS, d, Q_WIDTH + 2 * KV_WIDTH), d ** -0.5)
    attn_sinks = normal(ks[2], (N_ATTN_LAYERS, N_HEADS), 0.5)
    attn_w_out = normal(ks[3], (N_ATTN_LAYERS, Q_WIDTH, d), BETA * Q_WIDTH ** -0.5)
    conv_w_in = normal(ks[4], (N_CONV_LAYERS, d, 3 * CONV_CH), d ** -0.5)
    conv_w = normal(ks[5], (N_CONV_LAYERS, CONV_WIDTH, CONV_CH), CONV_WIDTH ** -0.5)
    conv_w_out = normal(ks[6], (N_CONV_LAYERS, CONV_CH, d), BETA * CONV_CH ** -0.5)
    ln_mix_g = 1.0 + normal(ks[7], (DEPTH, d), 0.02)
    ln_mix_b = normal(ks[8], (DEPTH, d), 0.02)
    ffn_w_gate_up = normal(ks[9], (DEPTH, d, 2 * FFN_HIDDEN), d ** -0.5)
    ffn_w_down = normal(ks[10], (DEPTH, FFN_HIDDEN, d), BETA * FFN_HIDDEN ** -0.5)
    ln_ffn_g = 1.0 + normal(ks[11], (DEPTH, d), 0.02)
    ln_ffn_b = normal(ks[12], (DEPTH, d), 0.02)
    return {
        "x": x,
        "attn_w_in": attn_w_in,
        "attn_sinks": attn_sinks,
        "attn_w_out": attn_w_out,
        "conv_w_in": conv_w_in,
        "conv_w": conv_w,
        "conv_w_out": conv_w_out,
        "ln_mix_g": ln_mix_g,
        "ln_mix_b": ln_mix_b,
        "ffn_w_gate_up": ffn_w_gate_up,
        "ffn_w_down": ffn_w_down,
        "ln_ffn_g": ln_ffn_g,
        "ln_ffn_b": ln_ffn_b,
    }


def reference(x, attn_w_in, attn_sinks, attn_w_out, conv_w_in, conv_w, conv_w_out,
              ln_mix_g, ln_mix_b, ffn_w_gate_up, ffn_w_down, ln_ffn_g, ln_ffn_b):
    for i in range(DEPTH):
        j = i // N_MIXERS
        if i % N_MIXERS == 0:
            mix = sliding_window_sink_attention(x, attn_w_in[j], attn_sinks[j], attn_w_out[j])
        else:
            mix = short_conv_mixer(x, conv_w_in[j], conv_w[j], conv_w_out[j])
        x = layer_norm(ALPHA * x + mix, ln_mix_g[i], ln_mix_b[i])
        ffn = swiglu_ffn(x, ffn_w_gate_up[i], ffn_w_down[i])
        x = layer_norm(ALPHA * x + ffn, ln_ffn_g[i], ln_ffn_b[i])
    return x
```

```python
import functools
import math

import jax
import jax.numpy as jnp
from jax import lax
from jax.experimental import pallas as pl
from jax.experimental.pallas import tpu as pltpu

DEPTH = 2
HEAD_DIM = 128
GROUP = 4
BLOCK = 128
CONV_WIDTH = 3
ALPHA = (2.0 * DEPTH) ** 0.25
LN_EPS = 1e-5

LANE = 128
BF16_SUBLANE = 16
VMEM_LIMIT_BYTES = 56 * 1024 * 1024

F32 = jnp.float32
BF16 = jnp.bfloat16


def _params(*semantics):
    return pltpu.CompilerParams(dimension_semantics=semantics,
                                vmem_limit_bytes=VMEM_LIMIT_BYTES)


def _mm_kernel(x_ref, w_ref, o_ref):
    o_ref[...] = jnp.dot(x_ref[...], w_ref[...],
                         preferred_element_type=F32).astype(o_ref.dtype)


def _matmul(x, w, *, bm, bn, out_dtype, name):
    m, k = x.shape
    n = w.shape[1]
    return pl.pallas_call(
        _mm_kernel,
        grid=(m // bm, pl.cdiv(n, bn)),
        in_specs=[pl.BlockSpec((bm, k), lambda i, j: (i, 0)),
                  pl.BlockSpec((k, bn), lambda i, j: (0, j))],
        out_specs=pl.BlockSpec((bm, bn), lambda i, j: (i, j)),
        out_shape=jax.ShapeDtypeStruct((m, n), out_dtype),
        compiler_params=_params("parallel", "arbitrary"),
        name=name,
    )(x, w)


def _swiglu_kernel(x_ref, wg_ref, wu_ref, o_ref):
    x = x_ref[...]
    gate = jnp.dot(x, wg_ref[...], preferred_element_type=F32)
    up = jnp.dot(x, wu_ref[...], preferred_element_type=F32)
    o_ref[...] = (gate * jax.nn.sigmoid(gate) * up).astype(o_ref.dtype)


def _swiglu_up(x, w_gate, w_up, *, bm, bn, name):
    m, k = x.shape
    h = w_gate.shape[1]
    return pl.pallas_call(
        _swiglu_kernel,
        grid=(m // bm, pl.cdiv(h, bn)),
        in_specs=[pl.BlockSpec((bm, k), lambda i, j: (i, 0)),
                  pl.BlockSpec((k, bn), lambda i, j: (0, j)),
                  pl.BlockSpec((k, bn), lambda i, j: (0, j))],
        out_specs=pl.BlockSpec((bm, bn), lambda i, j: (i, j)),
        out_shape=jax.ShapeDtypeStruct((m, h), BF16),
        compiler_params=_params("parallel", "arbitrary"),
        name=name,
    )(x, w_gate, w_up)


def _residual_ln_kernel(x_ref, mix_ref, g_ref, b_ref, o_ref, ob_ref):
    y = ALPHA * x_ref[...] + mix_ref[...]
    mu = jnp.mean(y, axis=-1, keepdims=True)
    yc = y - mu
    var = jnp.mean(yc * yc, axis=-1, keepdims=True)
    out = yc * lax.rsqrt(var + LN_EPS) * g_ref[...] + b_ref[...]
    o_ref[...] = out
    ob_ref[...] = out.astype(BF16)


def _residual_ln(x, mix, g, b, *, tm, name):
    m, d = x.shape
    row = pl.BlockSpec((tm, d), lambda i: (i, 0))
    vec = pl.BlockSpec((1, d), lambda i: (0, 0))
    return pl.pallas_call(
        _residual_ln_kernel,
        grid=(m // tm,),
        in_specs=[row, row, vec, vec],
        out_specs=[row, row],
        out_shape=[jax.ShapeDtypeStruct((m, d), F32),
                   jax.ShapeDtypeStruct((m, d), BF16)],
        compiler_params=_params("parallel"),
        name=name,
    )(x, mix, g.reshape(1, d), b.reshape(1, d))


def _attn_kernel(sinks_ref, q_ref, kc_ref, kp_ref, vc_ref, vp_ref, o_ref, *,
                 n_kv_heads, blocks_per_seq):
    i = pl.program_id(0)
    has_prev = (i % blocks_per_seq) != 0
    rows = GROUP * BLOCK
    qi = lax.broadcasted_iota(jnp.int32, (rows, 2 * BLOCK), 0) % BLOCK
    kk = lax.broadcasted_iota(jnp.int32, (rows, 2 * BLOCK), 1)
    valid_cur = (kk >= BLOCK) & (kk - BLOCK <= qi)
    valid_prev = (kk < BLOCK) & (kk > qi) & has_prev
    mask = valid_cur | valid_prev
    row_group = lax.broadcasted_iota(jnp.int32, (rows, 1), 0) // BLOCK
    scale = 1.0 / math.sqrt(HEAD_DIM)

    for h in range(n_kv_heads):
        ks = slice(h * HEAD_DIM, (h + 1) * HEAD_DIM)
        q = jnp.concatenate(
            [q_ref[:, (h * GROUP + g) * HEAD_DIM:(h * GROUP + g + 1) * HEAD_DIM]
             for g in range(GROUP)], axis=0)
        k = jnp.concatenate([kp_ref[:, ks], kc_ref[:, ks]], axis=0)
        v = jnp.concatenate([vp_ref[:, ks], vc_ref[:, ks]], axis=0)
        s = lax.dot_general(q, k, (((1,), (1,)), ((), ())),
                            preferred_element_type=F32) * scale
        s = jnp.where(mask, s, -jnp.inf)
        sink = jnp.zeros((rows, 1), F32)
        for g in range(GROUP):
            sink = jnp.where(row_group == g, sinks_ref[h * GROUP + g], sink)
        m = jnp.maximum(jnp.max(s, axis=-1, keepdims=True), sink)
        p = jnp.exp(s - m)
        denom = jnp.sum(p, axis=-1, keepdims=True) + jnp.exp(sink - m)
        probs = (p / denom).astype(BF16)
        o = jnp.dot(probs, v, preferred_element_type=F32)
        for g in range(GROUP):
            col = (h * GROUP + g) * HEAD_DIM
            o_ref[:, col:col + HEAD_DIM] = o[g * BLOCK:(g + 1) * BLOCK].astype(BF16)


def _attention(qkv, sinks, *, seq, n_heads, name):
    m = qkv.shape[0]
    n_kv_heads = n_heads // GROUP
    q_width = n_heads * HEAD_DIM
    kv_width = n_kv_heads * HEAD_DIM
    k_col = q_width // kv_width
    v_col = k_col + 1
    prev = lambda i: jnp.maximum(i - 1, 0)
    kernel = functools.partial(_attn_kernel, n_kv_heads=n_kv_heads,
                               blocks_per_seq=seq // BLOCK)
    return pl.pallas_call(
        kernel,
        grid=(m // BLOCK,),
        in_specs=[pl.BlockSpec(memory_space=pltpu.SMEM),
                  pl.BlockSpec((BLOCK, q_width), lambda i: (i, 0)),
                  pl.BlockSpec((BLOCK, kv_width), lambda i: (i, k_col)),
                  pl.BlockSpec((BLOCK, kv_width), lambda i: (prev(i), k_col)),
                  pl.BlockSpec((BLOCK, kv_width), lambda i: (i, v_col)),
                  pl.BlockSpec((BLOCK, kv_width), lambda i: (prev(i), v_col))],
        out_specs=pl.BlockSpec((BLOCK, q_width), lambda i: (i, 0)),
        out_shape=jax.ShapeDtypeStruct((m, q_width), BF16),
        compiler_params=_params("parallel"),
        name=name,
    )(sinks, qkv, qkv, qkv, qkv, qkv)


def _conv_gate_kernel(w_ref, b_ref, c_ref, h_ref, cp_ref, hp_ref, o_ref, u_ref, *,
                      tiles_per_seq):
    i = pl.program_id(0)
    tm = b_ref.shape[0]
    halo = cp_ref.shape[0]
    u_prev = cp_ref[...].astype(F32) * hp_ref[...].astype(F32)
    u_prev = jnp.where((i % tiles_per_seq) != 0, u_prev, 0.0)
    u_ref[0:halo, :] = u_prev
    u_ref[halo:halo + tm, :] = c_ref[...].astype(F32) * h_ref[...].astype(F32)
    conv = u_ref[halo - 2:halo - 2 + tm, :] * w_ref[0:1, :]
    conv = conv + u_ref[halo - 1:halo - 1 + tm, :] * w_ref[1:2, :]
    conv = conv + u_ref[halo:halo + tm, :] * w_ref[2:3, :]
    o_ref[...] = (b_ref[...].astype(F32) * conv).astype(BF16)


def _conv_gate(bch, conv_w, *, seq, tm, tn, name):
    m = bch.shape[0]
    c = conv_w.shape[1]
    nc = c // tn
    halo = BF16_SUBLANE
    hb = tm // halo
    prev = lambda i: jnp.maximum(i * hb - 1, 0)
    kernel = functools.partial(_conv_gate_kernel, tiles_per_seq=seq // tm)
    return pl.pallas_call(
        kernel,
        grid=(m // tm, nc),
        in_specs=[pl.BlockSpec((CONV_WIDTH, tn), lambda i, j: (0, j)),
                  pl.BlockSpec((tm, tn), lambda i, j: (i, j)),
                  pl.BlockSpec((tm, tn), lambda i, j: (i, nc + j)),
                  pl.BlockSpec((tm, tn), lambda i, j: (i, 2 * nc + j)),
                  pl.BlockSpec((halo, tn), lambda i, j: (prev(i), nc + j)),
                  pl.BlockSpec((halo, tn), lambda i, j: (prev(i), 2 * nc + j))],
        out_specs=pl.BlockSpec((tm, tn), lambda i, j: (i, j)),
        out_shape=jax.ShapeDtypeStruct((m, c), BF16),
        scratch_shapes=[pltpu.VMEM((halo + tm, tn), F32)],
        compiler_params=_params("parallel", "arbitrary"),
        name=name,
    )(conv_w, bch, bch, bch, bch, bch)


def _ffn(x, xb, w_gate_up, w_down, g, b, *, name):
    hidden = w_down.shape[0]
    w_gate = w_gate_up[:, :hidden].astype(BF16)
    w_up = w_gate_up[:, hidden:].astype(BF16)
    act = _swiglu_up(xb, w_gate, w_up, bm=1024, bn=512, name=name + "_up")
    ffn = _matmul(act, w_down.astype(BF16), bm=512, bn=256, out_dtype=F32,
                  name=name + "_down")
    return _residual_ln(x, ffn, g, b, tm=256, name=name + "_ln")


def kernel(x, attn_w_in, attn_sinks, attn_w_out, conv_w_in, conv_w, conv_w_out,
           ln_mix_g, ln_mix_b, ffn_w_gate_up, ffn_w_down, ln_ffn_g, ln_ffn_b):
    batch, seq, d = x.shape
    m = batch * seq
    n_heads = attn_sinks.shape[1]
    x = x.reshape(m, d)
    xb = x.astype(BF16)

    qkv = _matmul(xb, attn_w_in[0].astype(BF16), bm=1024, bn=512, out_dtype=BF16,
                  name="attn_qkv")
    attn = _attention(qkv, attn_sinks[0], seq=seq, n_heads=n_heads, name="attn_core")
    mix = _matmul(attn, attn_w_out[0].astype(BF16), bm=1024, bn=512, out_dtype=F32,
                  name="attn_out")
    x, xb = _residual_ln(x, mix, ln_mix_g[0], ln_mix_b[0], tm=256, name="attn_ln")
    x, xb = _ffn(x, xb, ffn_w_gate_up[0], ffn_w_down[0], ln_ffn_g[0], ln_ffn_b[0],
                 name="ffn0")

    bch = _matmul(xb, conv_w_in[0].astype(BF16), bm=1024, bn=512, out_dtype=BF16,
                  name="conv_in")
    gated = _conv_gate(bch, conv_w[0], seq=seq, tm=512, tn=512, name="conv_gate")
    mix = _matmul(gated, conv_w_out[0].astype(BF16), bm=1024, bn=512, out_dtype=F32,
                  name="conv_out")
    x, xb = _residual_ln(x, mix, ln_mix_g[1], ln_mix_b[1], tm=256, name="conv_ln")
    x, xb = _ffn(x, xb, ffn_w_gate_up[1], ffn_w_down[1], ln_ffn_g[1], ln_ffn_b[1],
                 name="ffn1")
    return x.reshape(batch, seq, d)
```

```python
import functools
import math

import jax
import jax.numpy as jnp
from jax import lax
from jax.experimental import pallas as pl
from jax.experimental.pallas import tpu as pltpu

DEPTH = 2
HEAD_DIM = 128
GROUP = 4
BLOCK = 128
CONV_WIDTH = 3
ALPHA = (2.0 * DEPTH) ** 0.25
LN_EPS = 1e-5

BF16_SUBLANE = 16
VMEM_LIMIT_BYTES = 58 * 1024 * 1024

MM_BM = 2048
MM_BN = 512
UP_BN = 256
DOWN_BM = 512
DOWN_BN = 512
LN_TM = 256
CONV_TM = 512
CONV_TN = 512

F32 = jnp.float32
BF16 = jnp.bfloat16


def _params(*semantics):
    return pltpu.CompilerParams(dimension_semantics=semantics,
                                vmem_limit_bytes=VMEM_LIMIT_BYTES)


def _resident_rows(bm, k):
    return pl.BlockSpec((bm, k), lambda i, j: (i, 0), pipeline_mode=pl.Buffered(1))


def _mm_kernel(x_ref, w_ref, o_ref):
    o_ref[...] = jnp.dot(x_ref[...], w_ref[...].astype(BF16),
                         preferred_element_type=F32).astype(o_ref.dtype)


def _matmul(x, w, *, bm, bn, out_dtype, resident, name):
    m, k = x.shape
    n = w.shape[1]
    x_spec = (_resident_rows(bm, k) if resident
              else pl.BlockSpec((bm, k), lambda i, j: (i, 0)))
    return pl.pallas_call(
        _mm_kernel,
        grid=(m // bm, n // bn),
        in_specs=[x_spec, pl.BlockSpec((k, bn), lambda i, j: (0, j))],
        out_specs=pl.BlockSpec((bm, bn), lambda i, j: (i, j)),
        out_shape=jax.ShapeDtypeStruct((m, n), out_dtype),
        compiler_params=_params("parallel", "arbitrary"),
        name=name,
    )(x, w)


def _swiglu_kernel(x_ref, wg_ref, wu_ref, wd_ref, o_ref, wdb_ref):
    x = x_ref[...]
    gate = jnp.dot(x, wg_ref[...].astype(BF16), preferred_element_type=F32)
    up = jnp.dot(x, wu_ref[...].astype(BF16), preferred_element_type=F32)
    o_ref[...] = (gate * jax.nn.sigmoid(gate) * up).astype(o_ref.dtype)
    wdb_ref[...] = wd_ref[...].astype(BF16)


def _swiglu_up(x, w_gate_up, w_down, *, bm, bn, name):
    m, k = x.shape
    hidden, d_out = w_down.shape
    ni, nj = m // bm, hidden // bn
    slab = hidden // (ni * nj)
    assert slab * ni * nj == hidden and slab % BF16_SUBLANE == 0
    slab_spec = pl.BlockSpec((slab, d_out), lambda i, j: (i * nj + j, 0))
    return pl.pallas_call(
        _swiglu_kernel,
        grid=(ni, nj),
        in_specs=[_resident_rows(bm, k),
                  pl.BlockSpec((k, bn), lambda i, j: (0, j)),
                  pl.BlockSpec((k, bn), lambda i, j: (0, nj + j)),
                  slab_spec],
        out_specs=[pl.BlockSpec((bm, bn), lambda i, j: (i, j)), slab_spec],
        out_shape=[jax.ShapeDtypeStruct((m, hidden), BF16),
                   jax.ShapeDtypeStruct((hidden, d_out), BF16)],
        compiler_params=_params("parallel", "arbitrary"),
        name=name,
    )(x, w_gate_up, w_gate_up, w_down)


def _residual_ln_kernel(x_ref, mix_ref, g_ref, b_ref, o_ref, *maybe_ob_ref):
    y = ALPHA * x_ref[...] + mix_ref[...]
    mu = jnp.mean(y, axis=-1, keepdims=True)
    yc = y - mu
    var = jnp.mean(yc * yc, axis=-1, keepdims=True)
    out = yc * lax.rsqrt(var + LN_EPS) * g_ref[...] + b_ref[...]
    o_ref[...] = out
    for ob_ref in maybe_ob_ref:
        ob_ref[...] = out.astype(BF16)


def _residual_ln(x, mix, g, b, *, tm, with_bf16, name):
    m, d = x.shape
    row = pl.BlockSpec((tm, d), lambda i: (i, 0))
    vec = pl.BlockSpec((1, d), lambda i: (0, 0))
    out_shape = [jax.ShapeDtypeStruct((m, d), F32)]
    if with_bf16:
        out_shape.append(jax.ShapeDtypeStruct((m, d), BF16))
    return pl.pallas_call(
        _residual_ln_kernel,
        grid=(m // tm,),
        in_specs=[row, row, vec, vec],
        out_specs=[row] * len(out_shape),
        out_shape=out_shape,
        compiler_params=_params("parallel"),
        name=name,
    )(x, mix, g.reshape(1, d), b.reshape(1, d))


def _attn_kernel(sinks_ref, q_ref, kc_ref, kp_ref, vc_ref, vp_ref, o_ref, *,
                 n_kv_heads, blocks_per_seq):
    i = pl.program_id(0)
    has_prev = (i % blocks_per_seq) != 0
    rows = GROUP * BLOCK
    qi = lax.broadcasted_iota(jnp.int32, (rows, 2 * BLOCK), 0) % BLOCK
    kk = lax.broadcasted_iota(jnp.int32, (rows, 2 * BLOCK), 1)
    valid_cur = (kk >= BLOCK) & (kk - BLOCK <= qi)
    valid_prev = (kk < BLOCK) & (kk > qi) & has_prev
    mask = valid_cur | valid_prev
    row_group = lax.broadcasted_iota(jnp.int32, (rows, 1), 0) // BLOCK
    scale = 1.0 / math.sqrt(HEAD_DIM)

    for h in range(n_kv_heads):
        ks = slice(h * HEAD_DIM, (h + 1) * HEAD_DIM)
        q = jnp.concatenate(
            [q_ref[:, (h * GROUP + g) * HEAD_DIM:(h * GROUP + g + 1) * HEAD_DIM]
             for g in range(GROUP)], axis=0)
        k = jnp.concatenate([kp_ref[:, ks], kc_ref[:, ks]], axis=0)
        v = jnp.concatenate([vp_ref[:, ks], vc_ref[:, ks]], axis=0)
        s = lax.dot_general(q, k, (((1,), (1,)), ((), ())),
                            preferred_element_type=F32) * scale
        s = jnp.where(mask, s, -jnp.inf)
        sink = jnp.zeros((rows, 1), F32)
        for g in range(GROUP):
            sink = jnp.where(row_group == g, sinks_ref[h * GROUP + g], sink)
        m = jnp.maximum(jnp.max(s, axis=-1, keepdims=True), sink)
        p = jnp.exp(s - m)
        denom = jnp.sum(p, axis=-1, keepdims=True) + jnp.exp(sink - m)
        probs = (p / denom).astype(BF16)
        o = jnp.dot(probs, v, preferred_element_type=F32)
        for g in range(GROUP):
            col = (h * GROUP + g) * HEAD_DIM
            o_ref[:, col:col + HEAD_DIM] = o[g * BLOCK:(g + 1) * BLOCK].astype(BF16)


def _attention(qkv, sinks, *, seq, n_heads, name):
    m = qkv.shape[0]
    n_kv_heads = n_heads // GROUP
    q_width = n_heads * HEAD_DIM
    kv_width = n_kv_heads * HEAD_DIM
    k_col = q_width // kv_width
    v_col = k_col + 1
    prev = lambda i: jnp.maximum(i - 1, 0)
    kernel = functools.partial(_attn_kernel, n_kv_heads=n_kv_heads,
                               blocks_per_seq=seq // BLOCK)
    return pl.pallas_call(
        kernel,
        grid=(m // BLOCK,),
        in_specs=[pl.BlockSpec(memory_space=pltpu.SMEM),
                  pl.BlockSpec((BLOCK, q_width), lambda i: (i, 0)),
                  pl.BlockSpec((BLOCK, kv_width), lambda i: (i, k_col)),
                  pl.BlockSpec((BLOCK, kv_width), lambda i: (prev(i), k_col)),
                  pl.BlockSpec((BLOCK, kv_width), lambda i: (i, v_col)),
                  pl.BlockSpec((BLOCK, kv_width), lambda i: (prev(i), v_col))],
        out_specs=pl.BlockSpec((BLOCK, q_width), lambda i: (i, 0)),
        out_shape=jax.ShapeDtypeStruct((m, q_width), BF16),
        compiler_params=_params("parallel"),
        name=name,
    )(sinks, qkv, qkv, qkv, qkv, qkv)


def _conv_gate_kernel(w_ref, b_ref, c_ref, h_ref, cp_ref, hp_ref, o_ref, u_ref, *,
                      tiles_per_seq):
    i = pl.program_id(0)
    tm = b_ref.shape[0]
    halo = cp_ref.shape[0]
    u_prev = cp_ref[...].astype(F32) * hp_ref[...].astype(F32)
    u_prev = jnp.where((i % tiles_per_seq) != 0, u_prev, 0.0)
    u_ref[0:halo, :] = u_prev
    u_ref[halo:halo + tm, :] = c_ref[...].astype(F32) * h_ref[...].astype(F32)
    conv = u_ref[halo - 2:halo - 2 + tm, :] * w_ref[0:1, :]
    conv = conv + u_ref[halo - 1:halo - 1 + tm, :] * w_ref[1:2, :]
    conv = conv + u_ref[halo:halo + tm, :] * w_ref[2:3, :]
    o_ref[...] = (b_ref[...].astype(F32) * conv).astype(BF16)


def _conv_gate(bch, conv_w, *, seq, tm, tn, name):
    m = bch.shape[0]
    c = conv_w.shape[1]
    nc = c // tn
    halo = BF16_SUBLANE
    hb = tm // halo
    prev = lambda i: jnp.maximum(i * hb - 1, 0)
    kernel = functools.partial(_conv_gate_kernel, tiles_per_seq=seq // tm)
    return pl.pallas_call(
        kernel,
        grid=(m // tm, nc),
        in_specs=[pl.BlockSpec((CONV_WIDTH, tn), lambda i, j: (0, j)),
                  pl.BlockSpec((tm, tn), lambda i, j: (i, j)),
                  pl.BlockSpec((tm, tn), lambda i, j: (i, nc + j)),
                  pl.BlockSpec((tm, tn), lambda i, j: (i, 2 * nc + j)),
                  pl.BlockSpec((halo, tn), lambda i, j: (prev(i), nc + j)),
                  pl.BlockSpec((halo, tn), lambda i, j: (prev(i), 2 * nc + j))],
        out_specs=pl.BlockSpec((tm, tn), lambda i, j: (i, j)),
        out_shape=jax.ShapeDtypeStruct((m, c), BF16),
        scratch_shapes=[pltpu.VMEM((halo + tm, tn), F32)],
        compiler_params=_params("parallel", "arbitrary"),
        name=name,
    )(conv_w, bch, bch, bch, bch, bch)


def _mm_d(x, w, out_dtype, name):
    return _matmul(x, w, bm=MM_BM, bn=MM_BN, out_dtype=out_dtype, resident=True,
                   name=name)


def _ffn(x, xb, w_gate_up, w_down, g, b, *, with_bf16, name):
    act, w_down_bf16 = _swiglu_up(xb, w_gate_up, w_down, bm=MM_BM, bn=UP_BN,
                                  name=name + "_up")
    ffn = _matmul(act, w_down_bf16, bm=DOWN_BM, bn=DOWN_BN, out_dtype=F32,
                  resident=False, name=name + "_down")
    return _residual_ln(x, ffn, g, b, tm=LN_TM, with_bf16=with_bf16, name=name + "_ln")


def kernel(x, attn_w_in, attn_sinks, attn_w_out, conv_w_in, conv_w, conv_w_out,
           ln_mix_g, ln_mix_b, ffn_w_gate_up, ffn_w_down, ln_ffn_g, ln_ffn_b):
    batch, seq, d = x.shape
    m = batch * seq
    n_heads = attn_sinks.shape[1]
    x = x.reshape(m, d)
    xb = x.astype(BF16)

    qkv = _mm_d(xb, attn_w_in[0], BF16, "attn_qkv")
    attn = _attention(qkv, attn_sinks[0], seq=seq, n_heads=n_heads, name="attn_core")
    mix = _mm_d(attn, attn_w_out[0], F32, "attn_out")
    x, xb = _residual_ln(x, mix, ln_mix_g[0], ln_mix_b[0], tm=LN_TM, with_bf16=True,
                         name="attn_ln")
    x, xb = _ffn(x, xb, ffn_w_gate_up[0], ffn_w_down[0], ln_ffn_g[0], ln_ffn_b[0],
                 with_bf16=True, name="ffn0")

    bch = _mm_d(xb, conv_w_in[0], BF16, "conv_in")
    gated = _conv_gate(bch, conv_w[0], seq=seq, tm=CONV_TM, tn=CONV_TN, name="conv_gate")
    mix = _mm_d(gated, conv_w_out[0], F32, "conv_out")
    x, xb = _residual_ln(x, mix, ln_mix_g[1], ln_mix_b[1], tm=LN_TM, with_bf16=True,
                         name="conv_ln")
    (x,) = _ffn(x, xb, ffn_w_gate_up[1], ffn_w_down[1], ln_ffn_g[1], ln_ffn_b[1],
                with_bf16=False, name="ffn1")
    return x.reshape(batch, seq, d)
```

```python
import functools
import math

import jax
import jax.numpy as jnp
from jax import lax
from jax.experimental import pallas as pl
from jax.experimental.pallas import tpu as pltpu

DEPTH = 2
HEAD_DIM = 128
GROUP = 4
BLOCK = 128
CONV_WIDTH = 3
ALPHA = (2.0 * DEPTH) ** 0.25
LN_EPS = 1e-5

BF16_SUBLANE = 16
VMEM_LIMIT_BYTES = 58 * 1024 * 1024

MM_BM = 2048
MM_BN = 512
UP_BN = 256
CONV_IN_BM = 1024
DOWN_BM = 512
DOWN_BN = 512
LN_TM = 256
CONV_TM = 512
CONV_TN = 2048

F32 = jnp.float32
BF16 = jnp.bfloat16


def _params(*semantics):
    return pltpu.CompilerParams(dimension_semantics=semantics,
                                vmem_limit_bytes=VMEM_LIMIT_BYTES)


def _resident_rows(bm, k):
    return pl.BlockSpec((bm, k), lambda i, j: (i, 0), pipeline_mode=pl.Buffered(1))


def _w_cols(w, layer, bn, first_block=0):
    k = w.shape[1]
    return pl.BlockSpec((None, k, bn), lambda i, j: (layer, 0, first_block + j))


def _mm_kernel(x_ref, w_ref, o_ref):
    o_ref[...] = jnp.dot(x_ref[...], w_ref[...].astype(BF16),
                         preferred_element_type=F32).astype(o_ref.dtype)


def _matmul(x, w, layer, *, bm, bn, out_dtype, resident, name):
    m, k = x.shape
    n = w.shape[2]
    x_spec = (_resident_rows(bm, k) if resident
              else pl.BlockSpec((bm, k), lambda i, j: (i, 0)))
    return pl.pallas_call(
        _mm_kernel,
        grid=(m // bm, n // bn),
        in_specs=[x_spec, _w_cols(w, layer, bn)],
        out_specs=pl.BlockSpec((bm, bn), lambda i, j: (i, j)),
        out_shape=jax.ShapeDtypeStruct((m, n), out_dtype),
        compiler_params=_params("parallel", "arbitrary"),
        name=name,
    )(x, w)


def _swiglu_kernel(x_ref, wg_ref, wu_ref, wd_ref, o_ref, wdb_ref):
    x = x_ref[...]
    gate = jnp.dot(x, wg_ref[...].astype(BF16), preferred_element_type=F32)
    up = jnp.dot(x, wu_ref[...].astype(BF16), preferred_element_type=F32)
    o_ref[...] = (gate * jax.nn.sigmoid(gate) * up).astype(o_ref.dtype)
    wdb_ref[...] = wd_ref[...].astype(BF16)


def _swiglu_up(x, w_gate_up, w_down, layer, *, bm, bn, name):
    m, k = x.shape
    _, hidden, d_out = w_down.shape
    ni, nj = m // bm, hidden // bn
    slab = hidden // (ni * nj)
    assert slab * ni * nj == hidden and slab % BF16_SUBLANE == 0
    return pl.pallas_call(
        _swiglu_kernel,
        grid=(ni, nj),
        in_specs=[_resident_rows(bm, k),
                  _w_cols(w_gate_up, layer, bn),
                  _w_cols(w_gate_up, layer, bn, first_block=nj),
                  pl.BlockSpec((None, slab, d_out), lambda i, j: (layer, i * nj + j, 0))],
        out_specs=[pl.BlockSpec((bm, bn), lambda i, j: (i, j)),
                   pl.BlockSpec((None, slab, d_out), lambda i, j: (0, i * nj + j, 0))],
        out_shape=[jax.ShapeDtypeStruct((m, hidden), BF16),
                   jax.ShapeDtypeStruct((1, hidden, d_out), BF16)],
        compiler_params=_params("parallel", "arbitrary"),
        name=name,
    )(x, w_gate_up, w_gate_up, w_down)


def _conv_in_kernel(x_ref, wb_ref, wc_ref, wh_ref, b_ref, u_ref):
    x = x_ref[...]
    b = jnp.dot(x, wb_ref[...].astype(BF16), preferred_element_type=F32)
    c = jnp.dot(x, wc_ref[...].astype(BF16), preferred_element_type=F32)
    h = jnp.dot(x, wh_ref[...].astype(BF16), preferred_element_type=F32)
    b_ref[...] = b.astype(BF16)
    u_ref[...] = (c * h).astype(BF16)


def _conv_in(x, w_in, layer, *, bm, bn, name):
    m, k = x.shape
    ch = w_in.shape[2] // 3
    nj = ch // bn
    out = pl.BlockSpec((bm, bn), lambda i, j: (i, j))
    return pl.pallas_call(
        _conv_in_kernel,
        grid=(m // bm, nj),
        in_specs=[_resident_rows(bm, k),
                  _w_cols(w_in, layer, bn),
                  _w_cols(w_in, layer, bn, first_block=nj),
                  _w_cols(w_in, layer, bn, first_block=2 * nj)],
        out_specs=[out, out],
        out_shape=[jax.ShapeDtypeStruct((m, ch), BF16)] * 2,
        compiler_params=_params("parallel", "arbitrary"),
        name=name,
    )(x, w_in, w_in, w_in)


def _residual_ln_kernel(x_ref, mix_ref, g_ref, b_ref, o_ref, *maybe_ob_ref):
    y = ALPHA * x_ref[...] + mix_ref[...]
    mu = jnp.mean(y, axis=-1, keepdims=True)
    yc = y - mu
    var = jnp.mean(yc * yc, axis=-1, keepdims=True)
    out = yc * lax.rsqrt(var + LN_EPS) * g_ref[...] + b_ref[...]
    o_ref[...] = out
    for ob_ref in maybe_ob_ref:
        ob_ref[...] = out.astype(BF16)


def _residual_ln(x, mix, g, b, *, tm, with_bf16, name):
    m, d = x.shape
    row = pl.BlockSpec((tm, d), lambda i: (i, 0))
    vec = pl.BlockSpec((1, d), lambda i: (0, 0))
    out_shape = [jax.ShapeDtypeStruct((m, d), F32)]
    if with_bf16:
        out_shape.append(jax.ShapeDtypeStruct((m, d), BF16))
    return pl.pallas_call(
        _residual_ln_kernel,
        grid=(m // tm,),
        in_specs=[row, row, vec, vec],
        out_specs=[row] * len(out_shape),
        out_shape=out_shape,
        compiler_params=_params("parallel"),
        name=name,
    )(x, mix, g.reshape(1, d), b.reshape(1, d))


def _attn_kernel(sinks_ref, q_ref, kc_ref, kp_ref, vc_ref, vp_ref, o_ref, *,
                 n_kv_heads, blocks_per_seq):
    i = pl.program_id(0)
    has_prev = (i % blocks_per_seq) != 0
    rows = GROUP * BLOCK
    qi = lax.broadcasted_iota(jnp.int32, (rows, 2 * BLOCK), 0) % BLOCK
    kk = lax.broadcasted_iota(jnp.int32, (rows, 2 * BLOCK), 1)
    valid_cur = (kk >= BLOCK) & (kk - BLOCK <= qi)
    valid_prev = (kk < BLOCK) & (kk > qi) & has_prev
    mask = valid_cur | valid_prev
    row_group = lax.broadcasted_iota(jnp.int32, (rows, 1), 0) // BLOCK
    scale = 1.0 / math.sqrt(HEAD_DIM)

    for h in range(n_kv_heads):
        ks = slice(h * HEAD_DIM, (h + 1) * HEAD_DIM)
        q = jnp.concatenate(
            [q_ref[:, (h * GROUP + g) * HEAD_DIM:(h * GROUP + g + 1) * HEAD_DIM]
             for g in range(GROUP)], axis=0)
        k = jnp.concatenate([kp_ref[:, ks], kc_ref[:, ks]], axis=0)
        v = jnp.concatenate([vp_ref[:, ks], vc_ref[:, ks]], axis=0)
        s = lax.dot_general(q, k, (((1,), (1,)), ((), ())),
                            preferred_element_type=F32) * scale
        s = jnp.where(mask, s, -jnp.inf)
        sink = jnp.zeros((rows, 1), F32)
        for g in range(GROUP):
            sink = jnp.where(row_group == g, sinks_ref[h * GROUP + g], sink)
        m = jnp.maximum(jnp.max(s, axis=-1, keepdims=True), sink)
        p = jnp.exp(s - m)
        denom = jnp.sum(p, axis=-1, keepdims=True) + jnp.exp(sink - m)
        probs = (p / denom).astype(BF16)
        o = jnp.dot(probs, v, preferred_element_type=F32)
        for g in range(GROUP):
            col = (h * GROUP + g) * HEAD_DIM
            o_ref[:, col:col + HEAD_DIM] = o[g * BLOCK:(g + 1) * BLOCK].astype(BF16)


def _attention(qkv, sinks, *, seq, n_heads, name):
    m = qkv.shape[0]
    n_kv_heads = n_heads // GROUP
    q_width = n_heads * HEAD_DIM
    kv_width = n_kv_heads * HEAD_DIM
    k_col = q_width // kv_width
    v_col = k_col + 1
    prev = lambda i: jnp.maximum(i - 1, 0)
    kernel = functools.partial(_attn_kernel, n_kv_heads=n_kv_heads,
                               blocks_per_seq=seq // BLOCK)
    return pl.pallas_call(
        kernel,
        grid=(m // BLOCK,),
        in_specs=[pl.BlockSpec(memory_space=pltpu.SMEM),
                  pl.BlockSpec((BLOCK, q_width), lambda i: (i, 0)),
                  pl.BlockSpec((BLOCK, kv_width), lambda i: (i, k_col)),
                  pl.BlockSpec((BLOCK, kv_width), lambda i: (prev(i), k_col)),
                  pl.BlockSpec((BLOCK, kv_width), lambda i: (i, v_col)),
                  pl.BlockSpec((BLOCK, kv_width), lambda i: (prev(i), v_col))],
        out_specs=pl.BlockSpec((BLOCK, q_width), lambda i: (i, 0)),
        out_shape=jax.ShapeDtypeStruct((m, q_width), BF16),
        compiler_params=_params("parallel"),
        name=name,
    )(sinks, qkv, qkv, qkv, qkv, qkv)


def _conv_gate_kernel(w_ref, b_ref, u_ref, up_ref, o_ref, ext_ref, *, tiles_per_seq):
    i = pl.program_id(0)
    tm = b_ref.shape[0]
    halo = up_ref.shape[0]
    ext_ref[0:halo, :] = jnp.where((i % tiles_per_seq) != 0, up_ref[...].astype(F32), 0.0)
    ext_ref[halo:halo + tm, :] = u_ref[...].astype(F32)
    conv = ext_ref[halo - 2:halo - 2 + tm, :] * w_ref[0:1, :]
    conv = conv + ext_ref[halo - 1:halo - 1 + tm, :] * w_ref[1:2, :]
    conv = conv + ext_ref[halo:halo + tm, :] * w_ref[2:3, :]
    o_ref[...] = (b_ref[...].astype(F32) * conv).astype(BF16)


def _conv_gate(b, u, conv_w, layer, *, seq, tm, tn, name):
    m, c = u.shape
    taps = conv_w.shape[1]
    halo = BF16_SUBLANE
    hb = tm // halo
    tile = pl.BlockSpec((tm, tn), lambda i, j: (i, j))
    kernel = functools.partial(_conv_gate_kernel, tiles_per_seq=seq // tm)
    return pl.pallas_call(
        kernel,
        grid=(m // tm, c // tn),
        in_specs=[pl.BlockSpec((None, taps, tn), lambda i, j: (layer, 0, j)),
                  tile, tile,
                  pl.BlockSpec((halo, tn), lambda i, j: (jnp.maximum(i * hb - 1, 0), j))],
        out_specs=tile,
        out_shape=jax.ShapeDtypeStruct((m, c), BF16),
        scratch_shapes=[pltpu.VMEM((halo + tm, tn), F32)],
        compiler_params=_params("parallel", "arbitrary"),
        name=name,
    )(conv_w, b, u, u)


def _mm_d(x, w, layer, out_dtype, name):
    return _matmul(x, w, layer, bm=MM_BM, bn=MM_BN, out_dtype=out_dtype, resident=True,
                   name=name)


def _ffn(x, xb, w_gate_up, w_down, layer, g, b, *, with_bf16, name):
    act, w_down_bf16 = _swiglu_up(xb, w_gate_up, w_down, layer, bm=MM_BM, bn=UP_BN,
                                  name=name + "_up")
    ffn = _matmul(act, w_down_bf16, 0, bm=DOWN_BM, bn=DOWN_BN, out_dtype=F32,
                  resident=False, name=name + "_down")
    return _residual_ln(x, ffn, g, b, tm=LN_TM, with_bf16=with_bf16, name=name + "_ln")


def kernel(x, attn_w_in, attn_sinks, attn_w_out, conv_w_in, conv_w, conv_w_out,
           ln_mix_g, ln_mix_b, ffn_w_gate_up, ffn_w_down, ln_ffn_g, ln_ffn_b):
    batch, seq, d = x.shape
    m = batch * seq
    n_heads = attn_sinks.shape[1]
    x = x.reshape(m, d)
    xb = x.astype(BF16)

    qkv = _mm_d(xb, attn_w_in, 0, BF16, "attn_qkv")
    attn = _attention(qkv, attn_sinks[0], seq=seq, n_heads=n_heads, name="attn_core")
    mix = _mm_d(attn, attn_w_out, 0, F32, "attn_out")
    x, xb = _residual_ln(x, mix, ln_mix_g[0], ln_mix_b[0], tm=LN_TM, with_bf16=True,
                         name="attn_ln")
    x, xb = _ffn(x, xb, ffn_w_gate_up, ffn_w_down, 0, ln_ffn_g[0], ln_ffn_b[0],
                 with_bf16=True, name="ffn0")

    gate_b, u = _conv_in(xb, conv_w_in, 0, bm=CONV_IN_BM, bn=UP_BN, name="conv_in")
    gated = _conv_gate(gate_b, u, conv_w, 0, seq=seq, tm=CONV_TM, tn=CONV_TN,
                       name="conv_gate")
    mix = _mm_d(gated, conv_w_out, 0, F32, "conv_out")
    x, xb = _residual_ln(x, mix, ln_mix_g[1], ln_mix_b[1], tm=LN_TM, with_bf16=True,
                         name="conv_ln")
    (x,) = _ffn(x, xb, ffn_w_gate_up, ffn_w_down, 1, ln_ffn_g[1], ln_ffn_b[1],
                with_bf16=False, name="ffn1")
    return x.reshape(batch, seq, d)
```

```python
import functools
import math

import jax
import jax.numpy as jnp
from jax import lax
from jax.experimental import pallas as pl
from jax.experimental.pallas import tpu as pltpu

DEPTH = 2
HEAD_DIM = 128
GROUP = 4
BLOCK = 128
CONV_WIDTH = 3
ALPHA = (2.0 * DEPTH) ** 0.25
LN_EPS = 1e-5

BF16_SUBLANE = 16
VMEM_LIMIT_BYTES = 58 * 1024 * 1024

MM_BM = 2048
MM_BN = 512
UP_BN = 256
CONV_IN_BM = 1024
DOWN_BM = 512
DOWN_BN = 512
LN_TM = 256
LANE = 128
CONV_TM = 512
CONV_TN = 2048

F32 = jnp.float32
BF16 = jnp.bfloat16


def _params(*semantics):
    return pltpu.CompilerParams(dimension_semantics=semantics,
                                vmem_limit_bytes=VMEM_LIMIT_BYTES)


def _resident_rows(bm, k):
    return pl.BlockSpec((bm, k), lambda i, j: (i, 0), pipeline_mode=pl.Buffered(1))


def _w_cols(w, layer, bn, first_block=0):
    k = w.shape[1]
    return pl.BlockSpec((None, k, bn), lambda i, j: (layer, 0, first_block + j))


def _mm_kernel(x_ref, w_ref, o_ref):
    o_ref[...] = jnp.dot(x_ref[...], w_ref[...].astype(BF16),
                         preferred_element_type=F32).astype(o_ref.dtype)


def _mm_residual_kernel(x_ref, w_ref, r_ref, o_ref):
    o_ref[...] = ALPHA * r_ref[...] + jnp.dot(x_ref[...], w_ref[...].astype(BF16),
                                              preferred_element_type=F32)


def _mm_ln_residual_kernel(x_ref, w_ref, y_ref, mu_ref, rstd_ref, g_ref, b_ref, o_ref):
    reps = y_ref.shape[1] // LANE
    mu = jnp.tile(mu_ref[...], (1, reps))
    rstd = jnp.tile(rstd_ref[...], (1, reps))
    r = (y_ref[...] - mu) * rstd * g_ref[...] + b_ref[...]
    o_ref[...] = ALPHA * r + jnp.dot(x_ref[...], w_ref[...].astype(BF16),
                                     preferred_element_type=F32)


def _matmul(x, w, layer, *, bm, bn, out_dtype, resident, name, residual=None,
            ln_residual=None):
    m, k = x.shape
    n = w.shape[2]
    x_spec = (_resident_rows(bm, k) if resident
              else pl.BlockSpec((bm, k), lambda i, j: (i, 0)))
    tile = pl.BlockSpec((bm, bn), lambda i, j: (i, j))
    in_specs, args, body = [x_spec, _w_cols(w, layer, bn)], [x, w], _mm_kernel
    if residual is not None:
        in_specs, args, body = in_specs + [tile], args + [residual], _mm_residual_kernel
    if ln_residual is not None:
        stat = pl.BlockSpec((bm, LANE), lambda i, j: (i, 0))
        vec = pl.BlockSpec((1, bn), lambda i, j: (0, j))
        in_specs = in_specs + [tile, stat, stat, vec, vec]
        args, body = args + list(ln_residual), _mm_ln_residual_kernel
    return pl.pallas_call(
        body,
        grid=(m // bm, n // bn),
        in_specs=in_specs,
        out_specs=tile,
        out_shape=jax.ShapeDtypeStruct((m, n), out_dtype),
        compiler_params=_params("parallel", "arbitrary"),
        name=name,
    )(*args)


def _swiglu_kernel(x_ref, wg_ref, wu_ref, wd_ref, o_ref, wdb_ref):
    x = x_ref[...]
    gate = jnp.dot(x, wg_ref[...].astype(BF16), preferred_element_type=F32)
    up = jnp.dot(x, wu_ref[...].astype(BF16), preferred_element_type=F32)
    o_ref[...] = (gate * jax.nn.sigmoid(gate) * up).astype(o_ref.dtype)
    wdb_ref[...] = wd_ref[...].astype(BF16)


def _swiglu_up(x, w_gate_up, w_down, layer, *, bm, bn, name):
    m, k = x.shape
    _, hidden, d_out = w_down.shape
    ni, nj = m // bm, hidden // bn
    slab = hidden // (ni * nj)
    assert slab * ni * nj == hidden and slab % BF16_SUBLANE == 0
    return pl.pallas_call(
        _swiglu_kernel,
        grid=(ni, nj),
        in_specs=[_resident_rows(bm, k),
                  _w_cols(w_gate_up, layer, bn),
                  _w_cols(w_gate_up, layer, bn, first_block=nj),
                  pl.BlockSpec((None, slab, d_out), lambda i, j: (layer, i * nj + j, 0))],
        out_specs=[pl.BlockSpec((bm, bn), lambda i, j: (i, j)),
                   pl.BlockSpec((None, slab, d_out), lambda i, j: (0, i * nj + j, 0))],
        out_shape=[jax.ShapeDtypeStruct((m, hidden), BF16),
                   jax.ShapeDtypeStruct((1, hidden, d_out), BF16)],
        compiler_params=_params("parallel", "arbitrary"),
        name=name,
    )(x, w_gate_up, w_gate_up, w_down)


def _conv_in_kernel(x_ref, wb_ref, wc_ref, wh_ref, b_ref, u_ref):
    x = x_ref[...]
    b = jnp.dot(x, wb_ref[...].astype(BF16), preferred_element_type=F32)
    c = jnp.dot(x, wc_ref[...].astype(BF16), preferred_element_type=F32)
    h = jnp.dot(x, wh_ref[...].astype(BF16), preferred_element_type=F32)
    b_ref[...] = b.astype(BF16)
    u_ref[...] = (c * h).astype(BF16)


def _conv_in(x, w_in, layer, *, bm, bn, name):
    m, k = x.shape
    ch = w_in.shape[2] // 3
    nj = ch // bn
    out = pl.BlockSpec((bm, bn), lambda i, j: (i, j))
    return pl.pallas_call(
        _conv_in_kernel,
        grid=(m // bm, nj),
        in_specs=[_resident_rows(bm, k),
                  _w_cols(w_in, layer, bn),
                  _w_cols(w_in, layer, bn, first_block=nj),
                  _w_cols(w_in, layer, bn, first_block=2 * nj)],
        out_specs=[out, out],
        out_shape=[jax.ShapeDtypeStruct((m, ch), BF16)] * 2,
        compiler_params=_params("parallel", "arbitrary"),
        name=name,
    )(x, w_in, w_in, w_in)


def _ln_kernel(y_ref, g_ref, b_ref, o_ref, *stat_refs):
    y = y_ref[...]
    mu = jnp.mean(y, axis=-1, keepdims=True)
    yc = y - mu
    var = jnp.mean(yc * yc, axis=-1, keepdims=True)
    rstd = lax.rsqrt(var + LN_EPS)
    o_ref[...] = (yc * rstd * g_ref[...] + b_ref[...]).astype(o_ref.dtype)
    if stat_refs:
        mu_ref, rstd_ref = stat_refs
        mu_ref[...] = jnp.broadcast_to(mu, mu_ref.shape)
        rstd_ref[...] = jnp.broadcast_to(rstd, rstd_ref.shape)


def _layer_norm(y, g, b, *, out_dtype, with_stats, tm, name):
    m, d = y.shape
    row = pl.BlockSpec((tm, d), lambda i: (i, 0))
    vec = pl.BlockSpec((1, d), lambda i: (0, 0))
    stat = pl.BlockSpec((tm, LANE), lambda i: (i, 0))
    out_specs = [row] + [stat, stat] * with_stats
    out_shape = ([jax.ShapeDtypeStruct((m, d), out_dtype)]
                 + [jax.ShapeDtypeStruct((m, LANE), F32)] * (2 * with_stats))
    return pl.pallas_call(
        _ln_kernel,
        grid=(m // tm,),
        in_specs=[row, vec, vec],
        out_specs=out_specs,
        out_shape=out_shape,
        compiler_params=_params("parallel"),
        name=name,
    )(y, g, b)


def _attn_kernel(sinks_ref, q_ref, kc_ref, kp_ref, vc_ref, vp_ref, o_ref, *,
                 n_kv_heads, blocks_per_seq):
    i = pl.program_id(0)
    has_prev = (i % blocks_per_seq) != 0
    rows = GROUP * BLOCK
    qi = lax.broadcasted_iota(jnp.int32, (rows, 2 * BLOCK), 0) % BLOCK
    kk = lax.broadcasted_iota(jnp.int32, (rows, 2 * BLOCK), 1)
    valid_cur = (kk >= BLOCK) & (kk - BLOCK <= qi)
    valid_prev = (kk < BLOCK) & (kk > qi) & has_prev
    mask = valid_cur | valid_prev
    row_group = lax.broadcasted_iota(jnp.int32, (rows, 1), 0) // BLOCK
    scale = 1.0 / math.sqrt(HEAD_DIM)

    for h in range(n_kv_heads):
        ks = slice(h * HEAD_DIM, (h + 1) * HEAD_DIM)
        q = jnp.concatenate(
            [q_ref[:, (h * GROUP + g) * HEAD_DIM:(h * GROUP + g + 1) * HEAD_DIM]
             for g in range(GROUP)], axis=0)
        k = jnp.concatenate([kp_ref[:, ks], kc_ref[:, ks]], axis=0)
        v = jnp.concatenate([vp_ref[:, ks], vc_ref[:, ks]], axis=0)
        s = lax.dot_general(q, k, (((1,), (1,)), ((), ())),
                            preferred_element_type=F32) * scale
        s = jnp.where(mask, s, -jnp.inf)
        sink = jnp.zeros((rows, 1), F32)
        for g in range(GROUP):
            sink = jnp.where(row_group == g, sinks_ref[h * GROUP + g], sink)
        m = jnp.maximum(jnp.max(s, axis=-1, keepdims=True), sink)
        p = jnp.exp(s - m)
        denom = jnp.sum(p, axis=-1, keepdims=True) + jnp.exp(sink - m)
        probs = (p / denom).astype(BF16)
        o = jnp.dot(probs, v, preferred_element_type=F32)
        for g in range(GROUP):
            col = (h * GROUP + g) * HEAD_DIM
            o_ref[:, col:col + HEAD_DIM] = o[g * BLOCK:(g + 1) * BLOCK].astype(BF16)


def _attention(qkv, sinks, *, seq, n_heads, name):
    m = qkv.shape[0]
    n_kv_heads = n_heads // GROUP
    q_width = n_heads * HEAD_DIM
    kv_width = n_kv_heads * HEAD_DIM
    k_col = q_width // kv_width
    v_col = k_col + 1
    prev = lambda i: jnp.maximum(i - 1, 0)
    kernel = functools.partial(_attn_kernel, n_kv_heads=n_kv_heads,
                               blocks_per_seq=seq // BLOCK)
    return pl.pallas_call(
        kernel,
        grid=(m // BLOCK,),
        in_specs=[pl.BlockSpec(memory_space=pltpu.SMEM),
                  pl.BlockSpec((BLOCK, q_width), lambda i: (i, 0)),
                  pl.BlockSpec((BLOCK, kv_width), lambda i: (i, k_col)),
                  pl.BlockSpec((BLOCK, kv_width), lambda i: (prev(i), k_col)),
                  pl.BlockSpec((BLOCK, kv_width), lambda i: (i, v_col)),
                  pl.BlockSpec((BLOCK, kv_width), lambda i: (prev(i), v_col))],
        out_specs=pl.BlockSpec((BLOCK, q_width), lambda i: (i, 0)),
        out_shape=jax.ShapeDtypeStruct((m, q_width), BF16),
        compiler_params=_params("parallel"),
        name=name,
    )(sinks, qkv, qkv, qkv, qkv, qkv)


def _conv_gate_kernel(w_ref, b_ref, u_ref, up_ref, o_ref, ext_ref, *, tiles_per_seq):
    i = pl.program_id(0)
    tm = b_ref.shape[0]
    halo = up_ref.shape[0]
    ext_ref[0:halo, :] = jnp.where((i % tiles_per_seq) != 0, up_ref[...].astype(F32), 0.0)
    ext_ref[halo:halo + tm, :] = u_ref[...].astype(F32)
    conv = ext_ref[halo - 2:halo - 2 + tm, :] * w_ref[0:1, :]
    conv = conv + ext_ref[halo - 1:halo - 1 + tm, :] * w_ref[1:2, :]
    conv = conv + ext_ref[halo:halo + tm, :] * w_ref[2:3, :]
    o_ref[...] = (b_ref[...].astype(F32) * conv).astype(BF16)


def _conv_gate(b, u, conv_w, layer, *, seq, tm, tn, name):
    m, c = u.shape
    taps = conv_w.shape[1]
    halo = BF16_SUBLANE
    hb = tm // halo
    tile = pl.BlockSpec((tm, tn), lambda i, j: (i, j))
    kernel = functools.partial(_conv_gate_kernel, tiles_per_seq=seq // tm)
    return pl.pallas_call(
        kernel,
        grid=(m // tm, c // tn),
        in_specs=[pl.BlockSpec((None, taps, tn), lambda i, j: (layer, 0, j)),
                  tile, tile,
                  pl.BlockSpec((halo, tn), lambda i, j: (jnp.maximum(i * hb - 1, 0), j))],
        out_specs=tile,
        out_shape=jax.ShapeDtypeStruct((m, c), BF16),
        scratch_shapes=[pltpu.VMEM((halo + tm, tn), F32)],
        compiler_params=_params("parallel", "arbitrary"),
        name=name,
    )(conv_w, b, u, u)


def _mm_d(x, w, layer, out_dtype, name, **residual):
    return _matmul(x, w, layer, bm=MM_BM, bn=MM_BN, out_dtype=out_dtype, resident=True,
                   name=name, **residual)


def _ffn(xb, w_gate_up, w_down, layer, ln_residual, *, name):
    act, w_down_bf16 = _swiglu_up(xb, w_gate_up, w_down, layer, bm=MM_BM, bn=UP_BN,
                                  name=name + "_up")
    return _matmul(act, w_down_bf16, 0, bm=DOWN_BM, bn=DOWN_BN, out_dtype=F32,
                   resident=False, ln_residual=ln_residual, name=name + "_down")


def kernel(x, attn_w_in, attn_sinks, attn_w_out, conv_w_in, conv_w, conv_w_out,
           ln_mix_g, ln_mix_b, ffn_w_gate_up, ffn_w_down, ln_ffn_g, ln_ffn_b):
    batch, seq, d = x.shape
    m = batch * seq
    n_heads = attn_sinks.shape[1]
    x = x.reshape(m, d)
    xb = x.astype(BF16)
    vec = lambda p, i: p[i].reshape(1, d)

    def norm(y, g, b, name):
        xb, mu, rstd = _layer_norm(y, g, b, out_dtype=BF16, with_stats=True, tm=LN_TM,
                                   name=name)
        return xb, (y, mu, rstd, g, b)


    qkv = _mm_d(xb, attn_w_in, 0, BF16, "attn_qkv")
    attn = _attention(qkv, attn_sinks[0], seq=seq, n_heads=n_heads, name="attn_core")
    y = _mm_d(attn, attn_w_out, 0, F32, "attn_out", residual=x)
    xb, res = norm(y, vec(ln_mix_g, 0), vec(ln_mix_b, 0), "attn_ln")
    y = _ffn(xb, ffn_w_gate_up, ffn_w_down, 0, res, name="ffn0")
    xb, res = norm(y, vec(ln_ffn_g, 0), vec(ln_ffn_b, 0), "ffn0_ln")

    gate_b, u = _conv_in(xb, conv_w_in, 0, bm=CONV_IN_BM, bn=UP_BN, name="conv_in")
    gated = _conv_gate(gate_b, u, conv_w, 0, seq=seq, tm=CONV_TM, tn=CONV_TN,
                       name="conv_gate")
    y = _mm_d(gated, conv_w_out, 0, F32, "conv_out", ln_residual=res)
    xb, res = norm(y, vec(ln_mix_g, 1), vec(ln_mix_b, 1), "conv_ln")
    y = _ffn(xb, ffn_w_gate_up, ffn_w_down, 1, res, name="ffn1")
    (out,) = _layer_norm(y, vec(ln_ffn_g, 1), vec(ln_ffn_b, 1), out_dtype=F32,
                         with_stats=False, tm=LN_TM, name="ffn1_ln")
    return out.reshape(batch, seq, d)
```

```python
import functools
import math

import jax
import jax.numpy as jnp
from jax import lax
from jax.experimental import pallas as pl
from jax.experimental.pallas import tpu as pltpu

DEPTH = 2
HEAD_DIM = 128
GROUP = 4
BLOCK = 128
CONV_WIDTH = 3
ALPHA = (2.0 * DEPTH) ** 0.25
LN_EPS = 1e-5
LOG2_E = math.log2(math.e)

BF16_SUBLANE = 16
VMEM_LIMIT_BYTES = 58 * 1024 * 1024

MM_BM = 2048
MM_BN = 512
UP_BN = 256
CONV_IN_BM = 2048
CONV_IN_CHUNK = 1024
DOWN_BM = 512
DOWN_BN = 512
LN_TM = 512
LANE = 128
CONV_TM = 512
CONV_TN = 2048

F32 = jnp.float32
BF16 = jnp.bfloat16


def _params(*semantics):
    return pltpu.CompilerParams(dimension_semantics=semantics,
                                vmem_limit_bytes=VMEM_LIMIT_BYTES)


def _resident_rows(bm, k):
    return pl.BlockSpec((bm, k), lambda i, j: (i, 0), pipeline_mode=pl.Buffered(1))


def _w_cols(w, layer, bn, first_block=0):
    k = w.shape[1]
    return pl.BlockSpec((None, k, bn), lambda i, j: (layer, 0, first_block + j))


def _mm_kernel(x_ref, w_ref, o_ref):
    o_ref[...] = jnp.dot(x_ref[...], w_ref[...].astype(BF16),
                         preferred_element_type=F32).astype(o_ref.dtype)


def _mm_residual_kernel(x_ref, w_ref, r_ref, o_ref):
    o_ref[...] = ALPHA * r_ref[...] + jnp.dot(x_ref[...], w_ref[...].astype(BF16),
                                              preferred_element_type=F32)


def _mm_ln_residual_kernel(x_ref, w_ref, y_ref, mu_ref, rstd_ref, g_ref, b_ref, o_ref):
    reps = y_ref.shape[1] // LANE
    mu = jnp.tile(mu_ref[...], (1, reps))
    rstd = jnp.tile(rstd_ref[...], (1, reps))
    r = (y_ref[...] - mu) * rstd * g_ref[...] + b_ref[...]
    o_ref[...] = ALPHA * r + jnp.dot(x_ref[...], w_ref[...].astype(BF16),
                                     preferred_element_type=F32)


def _matmul(x, w, layer, *, bm, bn, out_dtype, resident, name, residual=None,
            ln_residual=None):
    m, k = x.shape
    n = w.shape[2]
    x_spec = (_resident_rows(bm, k) if resident
              else pl.BlockSpec((bm, k), lambda i, j: (i, 0)))
    tile = pl.BlockSpec((bm, bn), lambda i, j: (i, j))
    in_specs, args, body = [x_spec, _w_cols(w, layer, bn)], [x, w], _mm_kernel
    if residual is not None:
        in_specs, args, body = in_specs + [tile], args + [residual], _mm_residual_kernel
    if ln_residual is not None:
        stat = pl.BlockSpec((bm, LANE), lambda i, j: (i, 0))
        vec = pl.BlockSpec((1, bn), lambda i, j: (0, j))
        in_specs = in_specs + [tile, stat, stat, vec, vec]
        args, body = args + list(ln_residual), _mm_ln_residual_kernel
    return pl.pallas_call(
        body,
        grid=(m // bm, n // bn),
        in_specs=in_specs,
        out_specs=tile,
        out_shape=jax.ShapeDtypeStruct((m, n), out_dtype),
        compiler_params=_params("parallel", "arbitrary"),
        name=name,
    )(*args)


def _swiglu_kernel(x_ref, wg_ref, wu_ref, wd_ref, o_ref, wdb_ref):
    x = x_ref[...]
    gate = jnp.dot(x, wg_ref[...].astype(BF16), preferred_element_type=F32)
    up = jnp.dot(x, wu_ref[...].astype(BF16), preferred_element_type=F32)
    o_ref[...] = (gate * jax.nn.sigmoid(gate) * up).astype(o_ref.dtype)
    wdb_ref[...] = wd_ref[...].astype(BF16)


def _swiglu_up(x, w_gate_up, w_down, layer, *, bm, bn, name):
    m, k = x.shape
    _, hidden, d_out = w_down.shape
    ni, nj = m // bm, hidden // bn
    slab = hidden // (ni * nj)
    assert slab * ni * nj == hidden and slab % BF16_SUBLANE == 0
    return pl.pallas_call(
        _swiglu_kernel,
        grid=(ni, nj),
        in_specs=[_resident_rows(bm, k),
                  _w_cols(w_gate_up, layer, bn),
                  _w_cols(w_gate_up, layer, bn, first_block=nj),
                  pl.BlockSpec((None, slab, d_out), lambda i, j: (layer, i * nj + j, 0))],
        out_specs=[pl.BlockSpec((bm, bn), lambda i, j: (i, j)),
                   pl.BlockSpec((None, slab, d_out), lambda i, j: (0, i * nj + j, 0))],
        out_shape=[jax.ShapeDtypeStruct((m, hidden), BF16),
                   jax.ShapeDtypeStruct((1, hidden, d_out), BF16)],
        compiler_params=_params("parallel", "arbitrary"),
        name=name,
    )(x, w_gate_up, w_gate_up, w_down)


def _conv_in_kernel(x_ref, wb_ref, wc_ref, wh_ref, b_ref, u_ref):
    wb = wb_ref[...].astype(BF16)
    wc = wc_ref[...].astype(BF16)
    wh = wh_ref[...].astype(BF16)
    chunk = CONV_IN_CHUNK
    for r in range(0, x_ref.shape[0], chunk):
        x = x_ref[r:r + chunk, :]
        b = jnp.dot(x, wb, preferred_element_type=F32)
        c = jnp.dot(x, wc, preferred_element_type=F32)
        h = jnp.dot(x, wh, preferred_element_type=F32)
        b_ref[r:r + chunk, :] = b.astype(BF16)
        u_ref[r:r + chunk, :] = (c * h).astype(BF16)


def _conv_in(x, w_in, layer, *, bm, bn, name):
    m, k = x.shape
    ch = w_in.shape[2] // 3
    nj = ch // bn
    out = pl.BlockSpec((bm, bn), lambda i, j: (i, j))
    return pl.pallas_call(
        _conv_in_kernel,
        grid=(m // bm, nj),
        in_specs=[_resident_rows(bm, k),
                  _w_cols(w_in, layer, bn),
                  _w_cols(w_in, layer, bn, first_block=nj),
                  _w_cols(w_in, layer, bn, first_block=2 * nj)],
        out_specs=[out, out],
        out_shape=[jax.ShapeDtypeStruct((m, ch), BF16)] * 2,
        compiler_params=_params("parallel", "arbitrary"),
        name=name,
    )(x, w_in, w_in, w_in)


def _ln_kernel(y_ref, g_ref, b_ref, o_ref, *stat_refs):
    y = y_ref[...]
    mu = jnp.mean(y, axis=-1, keepdims=True)
    yc = y - mu
    var = jnp.mean(yc * yc, axis=-1, keepdims=True)
    rstd = lax.rsqrt(var + LN_EPS)
    o_ref[...] = (yc * rstd * g_ref[...] + b_ref[...]).astype(o_ref.dtype)
    if stat_refs:
        mu_ref, rstd_ref = stat_refs
        mu_ref[...] = jnp.broadcast_to(mu, mu_ref.shape)
        rstd_ref[...] = jnp.broadcast_to(rstd, rstd_ref.shape)


def _layer_norm(y, g, b, *, out_dtype, with_stats, tm, name):
    m, d = y.shape
    row = pl.BlockSpec((tm, d), lambda i: (i, 0))
    vec = pl.BlockSpec((1, d), lambda i: (0, 0))
    stat = pl.BlockSpec((tm, LANE), lambda i: (i, 0))
    out_specs = [row] + [stat, stat] * with_stats
    out_shape = ([jax.ShapeDtypeStruct((m, d), out_dtype)]
                 + [jax.ShapeDtypeStruct((m, LANE), F32)] * (2 * with_stats))
    return pl.pallas_call(
        _ln_kernel,
        grid=(m // tm,),
        in_specs=[row, vec, vec],
        out_specs=out_specs,
        out_shape=out_shape,
        compiler_params=_params("parallel"),
        name=name,
    )(y, g, b)


def _attn_kernel(sinks_ref, q_ref, kc_ref, kp_ref, vc_ref, vp_ref, o_ref, *,
                 n_kv_heads, blocks_per_seq):
    i = pl.program_id(0)
    has_prev = (i % blocks_per_seq) != 0
    nq = GROUP * BLOCK
    kk = lax.broadcasted_iota(jnp.int32, (2 * BLOCK, nq), 0)
    qi = lax.broadcasted_iota(jnp.int32, (2 * BLOCK, nq), 1) % BLOCK
    valid_cur = (kk >= BLOCK) & (kk - BLOCK <= qi)
    valid_prev = (kk < BLOCK) & (kk > qi) & has_prev
    bias = jnp.where(valid_cur | valid_prev, 0.0, -jnp.inf).astype(F32)
    lane_group = lax.broadcasted_iota(jnp.int32, (1, nq), 1) // BLOCK
    scale2 = LOG2_E / math.sqrt(HEAD_DIM)

    for h in range(n_kv_heads):
        ks = slice(h * HEAD_DIM, (h + 1) * HEAD_DIM)
        q = jnp.concatenate(
            [q_ref[:, (h * GROUP + g) * HEAD_DIM:(h * GROUP + g + 1) * HEAD_DIM]
             for g in range(GROUP)], axis=0)
        k = jnp.concatenate([kp_ref[:, ks], kc_ref[:, ks]], axis=0)
        v = jnp.concatenate([vp_ref[:, ks], vc_ref[:, ks]], axis=0)
        st = lax.dot_general(k, q, (((1,), (1,)), ((), ())),
                             preferred_element_type=F32) * scale2 + bias
        sink = jnp.zeros((1, nq), F32)
        for g in range(GROUP):
            sink = jnp.where(lane_group == g, sinks_ref[h * GROUP + g] * LOG2_E, sink)
        m = jnp.maximum(jnp.max(st, axis=0, keepdims=True), sink)
        p = jnp.exp2(st - m)
        denom = jnp.sum(p, axis=0, keepdims=True) + jnp.exp2(sink - m)
        probs = (p / denom).astype(BF16)
        o = lax.dot_general(probs, v, (((0,), (0,)), ((), ())),
                            preferred_element_type=F32)
        for g in range(GROUP):
            col = (h * GROUP + g) * HEAD_DIM
            o_ref[:, col:col + HEAD_DIM] = o[g * BLOCK:(g + 1) * BLOCK].astype(BF16)


def _attention(qkv, sinks, *, seq, n_heads, name):
    m = qkv.shape[0]
    n_kv_heads = n_heads // GROUP
    q_width = n_heads * HEAD_DIM
    kv_width = n_kv_heads * HEAD_DIM
    k_col = q_width // kv_width
    v_col = k_col + 1
    prev = lambda i: jnp.maximum(i - 1, 0)
    kernel = functools.partial(_attn_kernel, n_kv_heads=n_kv_heads,
                               blocks_per_seq=seq // BLOCK)
    return pl.pallas_call(
        kernel,
        grid=(m // BLOCK,),
        in_specs=[pl.BlockSpec(memory_space=pltpu.SMEM),
                  pl.BlockSpec((BLOCK, q_width), lambda i: (i, 0)),
                  pl.BlockSpec((BLOCK, kv_width), lambda i: (i, k_col)),
                  pl.BlockSpec((BLOCK, kv_width), lambda i: (prev(i), k_col)),
                  pl.BlockSpec((BLOCK, kv_width), lambda i: (i, v_col)),
                  pl.BlockSpec((BLOCK, kv_width), lambda i: (prev(i), v_col))],
        out_specs=pl.BlockSpec((BLOCK, q_width), lambda i: (i, 0)),
        out_shape=jax.ShapeDtypeStruct((m, q_width), BF16),
        compiler_params=_params("parallel"),
        name=name,
    )(sinks, qkv, qkv, qkv, qkv, qkv)


def _conv_gate_kernel(w_ref, b_ref, u_ref, up_ref, o_ref, *, tiles_per_seq):
    i = pl.program_id(0)
    halo = up_ref.shape[0]
    w0, w1, w2 = w_ref[0:1, :], w_ref[1:2, :], w_ref[2:3, :]

    def gated(u, u1, u2, b):
        return (b.astype(F32) * (u2 * w0 + u1 * w1 + u * w2)).astype(BF16)

    u = u_ref[...].astype(F32)
    o_ref[...] = gated(u, pltpu.roll(u, 1, axis=0), pltpu.roll(u, 2, axis=0), b_ref[...])
    prev = jnp.where((i % tiles_per_seq) != 0, up_ref[...].astype(F32), 0.0)
    ext = jnp.concatenate([prev, u[0:halo]], axis=0)
    o_ref[0:halo, :] = gated(ext[halo:], pltpu.roll(ext, 1, axis=0)[halo:],
                             pltpu.roll(ext, 2, axis=0)[halo:], b_ref[0:halo, :])


def _conv_gate(b, u, conv_w, layer, *, seq, tm, tn, name):
    m, c = u.shape
    taps = conv_w.shape[1]
    halo = BF16_SUBLANE
    hb = tm // halo
    tile = pl.BlockSpec((tm, tn), lambda i, j: (i, j))
    kernel = functools.partial(_conv_gate_kernel, tiles_per_seq=seq // tm)
    return pl.pallas_call(
        kernel,
        grid=(m // tm, c // tn),
        in_specs=[pl.BlockSpec((None, taps, tn), lambda i, j: (layer, 0, j)),
                  tile, tile,
                  pl.BlockSpec((halo, tn), lambda i, j: (jnp.maximum(i * hb - 1, 0), j))],
        out_specs=tile,
        out_shape=jax.ShapeDtypeStruct((m, c), BF16),
        compiler_params=_params("parallel", "arbitrary"),
        name=name,
    )(conv_w, b, u, u)


def _mm_d(x, w, layer, out_dtype, name, **residual):
    return _matmul(x, w, layer, bm=MM_BM, bn=MM_BN, out_dtype=out_dtype, resident=True,
                   name=name, **residual)


def _ffn(xb, w_gate_up, w_down, layer, ln_residual, *, name):
    act, w_down_bf16 = _swiglu_up(xb, w_gate_up, w_down, layer, bm=MM_BM, bn=UP_BN,
                                  name=name + "_up")
    return _matmul(act, w_down_bf16, 0, bm=DOWN_BM, bn=DOWN_BN, out_dtype=F32,
                   resident=False, ln_residual=ln_residual, name=name + "_down")


def kernel(x, attn_w_in, attn_sinks, attn_w_out, conv_w_in, conv_w, conv_w_out,
           ln_mix_g, ln_mix_b, ffn_w_gate_up, ffn_w_down, ln_ffn_g, ln_ffn_b):
    batch, seq, d = x.shape
    m = batch * seq
    n_heads = attn_sinks.shape[1]
    x = x.reshape(m, d)
    xb = x.astype(BF16)
    vec = lambda p, i: p[i].reshape(1, d)

    def norm(y, g, b, name):
        xb, mu, rstd = _layer_norm(y, g, b, out_dtype=BF16, with_stats=True, tm=LN_TM,
                                   name=name)
        return xb, (y, mu, rstd, g, b)


    qkv = _mm_d(xb, attn_w_in, 0, BF16, "attn_qkv")
    attn = _attention(qkv, attn_sinks[0], seq=seq, n_heads=n_heads, name="attn_core")
    y = _mm_d(attn, attn_w_out, 0, F32, "attn_out", residual=x)
    xb, res = norm(y, vec(ln_mix_g, 0), vec(ln_mix_b, 0), "attn_ln")
    y = _ffn(xb, ffn_w_gate_up, ffn_w_down, 0, res, name="ffn0")
    xb, res = norm(y, vec(ln_ffn_g, 0), vec(ln_ffn_b, 0), "ffn0_ln")

    gate_b, u = _conv_in(xb, conv_w_in, 0, bm=CONV_IN_BM, bn=UP_BN, name="conv_in")
    gated = _conv_gate(gate_b, u, conv_w, 0, seq=seq, tm=CONV_TM, tn=CONV_TN,
                       name="conv_gate")
    y = _mm_d(gated, conv_w_out, 0, F32, "conv_out", ln_residual=res)
    xb, res = norm(y, vec(ln_mix_g, 1), vec(ln_mix_b, 1), "conv_ln")
    y = _ffn(xb, ffn_w_gate_up, ffn_w_down, 1, res, name="ffn1")
    (out,) = _layer_norm(y, vec(ln_ffn_g, 1), vec(ln_ffn_b, 1), out_dtype=F32,
                         with_stats=False, tm=LN_TM, name="ffn1_ln")
    return out.reshape(batch, seq, d)
```

```python
import functools
import math

import jax
import jax.numpy as jnp
from jax import lax
from jax.experimental import pallas as pl
from jax.experimental.pallas import tpu as pltpu

DEPTH = 2
HEAD_DIM = 128
GROUP = 4
BLOCK = 128
CONV_WIDTH = 3
ALPHA = (2.0 * DEPTH) ** 0.25
LN_EPS = 1e-5
LOG2_E = math.log2(math.e)

BF16_SUBLANE = 16
VMEM_LIMIT_BYTES = 58 * 1024 * 1024

MM_BM = 2048
MM_BN = 512
UP_BN = 256
CONV_IN_BM = 2048
ROW_CHUNK = 1024
DOWN_BM = 512
DOWN_BN = 512
LN_TM = 512
LANE = 128
CONV_TM = 512
CONV_TN = 2048

F32 = jnp.float32
BF16 = jnp.bfloat16


def _params(*semantics):
    return pltpu.CompilerParams(dimension_semantics=semantics,
                                vmem_limit_bytes=VMEM_LIMIT_BYTES)


def _resident_rows(bm, k):
    return pl.BlockSpec((bm, k), lambda i, j: (i, 0), pipeline_mode=pl.Buffered(1))


def _w_cols(w, layer, bn, first_block=0):
    k = w.shape[1]
    return pl.BlockSpec((None, k, bn), lambda i, j: (layer, 0, first_block + j))


def _dot_row_chunks(x_ref, w, emit):
    chunk = min(ROW_CHUNK, x_ref.shape[0])
    for r in range(0, x_ref.shape[0], chunk):
        rows = slice(r, r + chunk)
        emit(rows, jnp.dot(x_ref[rows, :], w, preferred_element_type=F32))


def _mm_kernel(x_ref, w_ref, o_ref):
    def emit(rows, acc):
        o_ref[rows, :] = acc.astype(o_ref.dtype)
    _dot_row_chunks(x_ref, w_ref[...].astype(BF16), emit)


def _mm_residual_kernel(x_ref, w_ref, r_ref, o_ref):
    def emit(rows, acc):
        o_ref[rows, :] = ALPHA * r_ref[rows, :] + acc
    _dot_row_chunks(x_ref, w_ref[...].astype(BF16), emit)


def _mm_ln_residual_kernel(x_ref, w_ref, y_ref, mu_ref, rstd_ref, g_ref, b_ref, o_ref):
    reps = y_ref.shape[1] // LANE

    def emit(rows, acc):
        mu = jnp.tile(mu_ref[rows, :], (1, reps))
        rstd = jnp.tile(rstd_ref[rows, :], (1, reps))
        r = (y_ref[rows, :] - mu) * rstd * g_ref[...] + b_ref[...]
        o_ref[rows, :] = ALPHA * r + acc
    _dot_row_chunks(x_ref, w_ref[...].astype(BF16), emit)


def _matmul(x, w, layer, *, bm, bn, out_dtype, resident, name, residual=None,
            ln_residual=None):
    m, k = x.shape
    n = w.shape[2]
    x_spec = (_resident_rows(bm, k) if resident
              else pl.BlockSpec((bm, k), lambda i, j: (i, 0)))
    tile = pl.BlockSpec((bm, bn), lambda i, j: (i, j))
    in_specs, args, body = [x_spec, _w_cols(w, layer, bn)], [x, w], _mm_kernel
    if residual is not None:
        in_specs, args, body = in_specs + [tile], args + [residual], _mm_residual_kernel
    if ln_residual is not None:
        stat = pl.BlockSpec((bm, LANE), lambda i, j: (i, 0))
        vec = pl.BlockSpec((1, bn), lambda i, j: (0, j))
        in_specs = in_specs + [tile, stat, stat, vec, vec]
        args, body = args + list(ln_residual), _mm_ln_residual_kernel
    return pl.pallas_call(
        body,
        grid=(m // bm, n // bn),
        in_specs=in_specs,
        out_specs=tile,
        out_shape=jax.ShapeDtypeStruct((m, n), out_dtype),
        compiler_params=_params("parallel", "arbitrary"),
        name=name,
    )(*args)


def _swiglu_kernel(x_ref, wg_ref, wu_ref, wd_ref, o_ref, wdb_ref):
    wg = wg_ref[...].astype(BF16)
    wu = wu_ref[...].astype(BF16)
    for r in range(0, x_ref.shape[0], ROW_CHUNK):
        x = x_ref[r:r + ROW_CHUNK, :]
        gate = jnp.dot(x, wg, preferred_element_type=F32)
        up = jnp.dot(x, wu, preferred_element_type=F32)
        o_ref[r:r + ROW_CHUNK, :] = (gate * jax.nn.sigmoid(gate) * up).astype(o_ref.dtype)
    wdb_ref[...] = wd_ref[...].astype(BF16)


def _swiglu_up(x, w_gate_up, w_down, layer, *, bm, bn, name):
    m, k = x.shape
    _, hidden, d_out = w_down.shape
    ni, nj = m // bm, hidden // bn
    slab = hidden // (ni * nj)
    assert slab * ni * nj == hidden and slab % BF16_SUBLANE == 0
    return pl.pallas_call(
        _swiglu_kernel,
        grid=(ni, nj),
        in_specs=[pl.BlockSpec((bm, k), lambda i, j: (i, 0)),
                  _w_cols(w_gate_up, layer, bn),
                  _w_cols(w_gate_up, layer, bn, first_block=nj),
                  pl.BlockSpec((None, slab, d_out), lambda i, j: (layer, i * nj + j, 0))],
        out_specs=[pl.BlockSpec((bm, bn), lambda i, j: (i, j)),
                   pl.BlockSpec((None, slab, d_out), lambda i, j: (0, i * nj + j, 0))],
        out_shape=[jax.ShapeDtypeStruct((m, hidden), BF16),
                   jax.ShapeDtypeStruct((1, hidden, d_out), BF16)],
        compiler_params=_params("parallel", "arbitrary"),
        name=name,
    )(x, w_gate_up, w_gate_up, w_down)


def _conv_in_kernel(x_ref, wb_ref, wc_ref, wh_ref, b_ref, u_ref):
    wb = wb_ref[...].astype(BF16)
    wc = wc_ref[...].astype(BF16)
    wh = wh_ref[...].astype(BF16)
    chunk = ROW_CHUNK
    for r in range(0, x_ref.shape[0], chunk):
        x = x_ref[r:r + chunk, :]
        b = jnp.dot(x, wb, preferred_element_type=F32)
        c = jnp.dot(x, wc, preferred_element_type=F32)
        h = jnp.dot(x, wh, preferred_element_type=F32)
        b_ref[r:r + chunk, :] = b.astype(BF16)
        u_ref[r:r + chunk, :] = (c * h).astype(BF16)


def _conv_in(x, w_in, layer, *, bm, bn, name):
    m, k = x.shape
    ch = w_in.shape[2] // 3
    nj = ch // bn
    out = pl.BlockSpec((bm, bn), lambda i, j: (i, j))
    return pl.pallas_call(
        _conv_in_kernel,
        grid=(m // bm, nj),
        in_specs=[_resident_rows(bm, k),
                  _w_cols(w_in, layer, bn),
                  _w_cols(w_in, layer, bn, first_block=nj),
                  _w_cols(w_in, layer, bn, first_block=2 * nj)],
        out_specs=[out, out],
        out_shape=[jax.ShapeDtypeStruct((m, ch), BF16)] * 2,
        compiler_params=_params("parallel", "arbitrary"),
        name=name,
    )(x, w_in, w_in, w_in)


def _ln_kernel(y_ref, g_ref, b_ref, o_ref, *stat_refs):
    y = y_ref[...]
    mu = jnp.mean(y, axis=-1, keepdims=True)
    yc = y - mu
    var = jnp.mean(yc * yc, axis=-1, keepdims=True)
    rstd = lax.rsqrt(var + LN_EPS)
    o_ref[...] = (yc * rstd * g_ref[...] + b_ref[...]).astype(o_ref.dtype)
    if stat_refs:
        mu_ref, rstd_ref = stat_refs
        mu_ref[...] = jnp.broadcast_to(mu, mu_ref.shape)
        rstd_ref[...] = jnp.broadcast_to(rstd, rstd_ref.shape)


def _layer_norm(y, g, b, *, out_dtype, with_stats, tm, name):
    m, d = y.shape
    row = pl.BlockSpec((tm, d), lambda i: (i, 0))
    vec = pl.BlockSpec((1, d), lambda i: (0, 0))
    stat = pl.BlockSpec((tm, LANE), lambda i: (i, 0))
    out_specs = [row] + [stat, stat] * with_stats
    out_shape = ([jax.ShapeDtypeStruct((m, d), out_dtype)]
                 + [jax.ShapeDtypeStruct((m, LANE), F32)] * (2 * with_stats))
    return pl.pallas_call(
        _ln_kernel,
        grid=(m // tm,),
        in_specs=[row, vec, vec],
        out_specs=out_specs,
        out_shape=out_shape,
        compiler_params=_params("parallel"),
        name=name,
    )(y, g, b)


def _attn_kernel(sinks_ref, q_ref, kc_ref, kp_ref, vc_ref, vp_ref, o_ref, *,
                 n_kv_heads, blocks_per_seq):
    i = pl.program_id(0)
    has_prev = (i % blocks_per_seq) != 0
    nq = GROUP * BLOCK
    kk = lax.broadcasted_iota(jnp.int32, (2 * BLOCK, nq), 0)
    qi = lax.broadcasted_iota(jnp.int32, (2 * BLOCK, nq), 1) % BLOCK
    valid_cur = (kk >= BLOCK) & (kk - BLOCK <= qi)
    valid_prev = (kk < BLOCK) & (kk > qi) & has_prev
    bias = jnp.where(valid_cur | valid_prev, 0.0, -jnp.inf).astype(F32)
    lane_group = lax.broadcasted_iota(jnp.int32, (1, nq), 1) // BLOCK
    scale2 = LOG2_E / math.sqrt(HEAD_DIM)

    for h in range(n_kv_heads):
        ks = slice(h * HEAD_DIM, (h + 1) * HEAD_DIM)
        q = jnp.concatenate(
            [q_ref[:, (h * GROUP + g) * HEAD_DIM:(h * GROUP + g + 1) * HEAD_DIM]
             for g in range(GROUP)], axis=0)
        k = jnp.concatenate([kp_ref[:, ks], kc_ref[:, ks]], axis=0)
        v = jnp.concatenate([vp_ref[:, ks], vc_ref[:, ks]], axis=0)
        st = lax.dot_general(k, q, (((1,), (1,)), ((), ())),
                             preferred_element_type=F32) * scale2 + bias
        sink = jnp.zeros((1, nq), F32)
        for g in range(GROUP):
            sink = jnp.where(lane_group == g, sinks_ref[h * GROUP + g] * LOG2_E, sink)
        m = jnp.maximum(jnp.max(st, axis=0, keepdims=True), sink)
        p = jnp.exp2(st - m)
        denom = jnp.sum(p, axis=0, keepdims=True) + jnp.exp2(sink - m)
        probs = (p / denom).astype(BF16)
        o = lax.dot_general(probs, v, (((0,), (0,)), ((), ())),
                            preferred_element_type=F32)
        for g in range(GROUP):
            col = (h * GROUP + g) * HEAD_DIM
            o_ref[:, col:col + HEAD_DIM] = o[g * BLOCK:(g + 1) * BLOCK].astype(BF16)


def _attention(qkv, sinks, *, seq, n_heads, name):
    m = qkv.shape[0]
    n_kv_heads = n_heads // GROUP
    q_width = n_heads * HEAD_DIM
    kv_width = n_kv_heads * HEAD_DIM
    k_col = q_width // kv_width
    v_col = k_col + 1
    prev = lambda i: jnp.maximum(i - 1, 0)
    kernel = functools.partial(_attn_kernel, n_kv_heads=n_kv_heads,
                               blocks_per_seq=seq // BLOCK)
    return pl.pallas_call(
        kernel,
        grid=(m // BLOCK,),
        in_specs=[pl.BlockSpec(memory_space=pltpu.SMEM),
                  pl.BlockSpec((BLOCK, q_width), lambda i: (i, 0)),
                  pl.BlockSpec((BLOCK, kv_width), lambda i: (i, k_col)),
                  pl.BlockSpec((BLOCK, kv_width), lambda i: (prev(i), k_col)),
                  pl.BlockSpec((BLOCK, kv_width), lambda i: (i, v_col)),
                  pl.BlockSpec((BLOCK, kv_width), lambda i: (prev(i), v_col))],
        out_specs=pl.BlockSpec((BLOCK, q_width), lambda i: (i, 0)),
        out_shape=jax.ShapeDtypeStruct((m, q_width), BF16),
        compiler_params=_params("parallel"),
        name=name,
    )(sinks, qkv, qkv, qkv, qkv, qkv)


def _conv_gate_kernel(w_ref, b_ref, u_ref, up_ref, o_ref, *, tiles_per_seq):
    i = pl.program_id(0)
    halo = up_ref.shape[0]
    w0, w1, w2 = w_ref[0:1, :], w_ref[1:2, :], w_ref[2:3, :]

    def gated(u, u1, u2, b):
        return (b.astype(F32) * (u2 * w0 + u1 * w1 + u * w2)).astype(BF16)

    u = u_ref[...].astype(F32)
    o_ref[...] = gated(u, pltpu.roll(u, 1, axis=0), pltpu.roll(u, 2, axis=0), b_ref[...])
    prev = jnp.where((i % tiles_per_seq) != 0, up_ref[...].astype(F32), 0.0)
    ext = jnp.concatenate([prev, u[0:halo]], axis=0)
    o_ref[0:halo, :] = gated(ext[halo:], pltpu.roll(ext, 1, axis=0)[halo:],
                             pltpu.roll(ext, 2, axis=0)[halo:], b_ref[0:halo, :])


def _conv_gate(b, u, conv_w, layer, *, seq, tm, tn, name):
    m, c = u.shape
    taps = conv_w.shape[1]
    halo = BF16_SUBLANE
    hb = tm // halo
    tile = pl.BlockSpec((tm, tn), lambda i, j: (i, j))
    kernel = functools.partial(_conv_gate_kernel, tiles_per_seq=seq // tm)
    return pl.pallas_call(
        kernel,
        grid=(m // tm, c // tn),
        in_specs=[pl.BlockSpec((None, taps, tn), lambda i, j: (layer, 0, j)),
                  tile, tile,
                  pl.BlockSpec((halo, tn), lambda i, j: (jnp.maximum(i * hb - 1, 0), j))],
        out_specs=tile,
        out_shape=jax.ShapeDtypeStruct((m, c), BF16),
        compiler_params=_params("parallel", "arbitrary"),
        name=name,
    )(conv_w, b, u, u)


def _mm_d(x, w, layer, out_dtype, name, **residual):
    return _matmul(x, w, layer, bm=MM_BM, bn=MM_BN, out_dtype=out_dtype,
                   resident=bool(residual), name=name, **residual)


def _ffn(xb, w_gate_up, w_down, layer, ln_residual, *, name):
    act, w_down_bf16 = _swiglu_up(xb, w_gate_up, w_down, layer, bm=MM_BM, bn=UP_BN,
                                  name=name + "_up")
    return _matmul(act, w_down_bf16, 0, bm=DOWN_BM, bn=DOWN_BN, out_dtype=F32,
                   resident=False, ln_residual=ln_residual, name=name + "_down")


def kernel(x, attn_w_in, attn_sinks, attn_w_out, conv_w_in, conv_w, conv_w_out,
           ln_mix_g, ln_mix_b, ffn_w_gate_up, ffn_w_down, ln_ffn_g, ln_ffn_b):
    batch, seq, d = x.shape
    m = batch * seq
    n_heads = attn_sinks.shape[1]
    x = x.reshape(m, d)
    xb = x.astype(BF16)
    vec = lambda p, i: p[i].reshape(1, d)

    def norm(y, g, b, name):
        xb, mu, rstd = _layer_norm(y, g, b, out_dtype=BF16, with_stats=True, tm=LN_TM,
                                   name=name)
        return xb, (y, mu, rstd, g, b)


    qkv = _mm_d(xb, attn_w_in, 0, BF16, "attn_qkv")
    attn = _attention(qkv, attn_sinks[0], seq=seq, n_heads=n_heads, name="attn_core")
    y = _mm_d(attn, attn_w_out, 0, F32, "attn_out", residual=x)
    xb, res = norm(y, vec(ln_mix_g, 0), vec(ln_mix_b, 0), "attn_ln")
    y = _ffn(xb, ffn_w_gate_up, ffn_w_down, 0, res, name="ffn0")
    xb, res = norm(y, vec(ln_ffn_g, 0), vec(ln_ffn_b, 0), "ffn0_ln")

    gate_b, u = _conv_in(xb, conv_w_in, 0, bm=CONV_IN_BM, bn=UP_BN, name="conv_in")
    gated = _conv_gate(gate_b, u, conv_w, 0, seq=seq, tm=CONV_TM, tn=CONV_TN,
                       name="conv_gate")
    y = _mm_d(gated, conv_w_out, 0, F32, "conv_out", ln_residual=res)
    xb, res = norm(y, vec(ln_mix_g, 1), vec(ln_mix_b, 1), "conv_ln")
    y = _ffn(xb, ffn_w_gate_up, ffn_w_down, 1, res, name="ffn1")
    (out,) = _layer_norm(y, vec(ln_ffn_g, 1), vec(ln_ffn_b, 1), out_dtype=F32,
                         with_stats=False, tm=LN_TM, name="ffn1_ln")
    return out.reshape(batch, seq, d)
```

```python
import functools
import math

import jax
import jax.numpy as jnp
from jax import lax
from jax.experimental import pallas as pl
from jax.experimental.pallas import tpu as pltpu

DEPTH = 2
HEAD_DIM = 128
GROUP = 4
BLOCK = 128
CONV_WIDTH = 3
ALPHA = (2.0 * DEPTH) ** 0.25
LN_EPS = 1e-5
LOG2_E = math.log2(math.e)

BF16_SUBLANE = 16
VMEM_LIMIT_BYTES = 58 * 1024 * 1024

MM_BM = 2048
MM_BN = 512
UP_BN = 256
CONV_IN_BM = 2048
ROW_CHUNK = 1024
DOWN_BM = 1024
DOWN_BN = 512
LN_TM = 512
LANE = 128

F32 = jnp.float32
BF16 = jnp.bfloat16


def _params(*semantics):
    return pltpu.CompilerParams(dimension_semantics=semantics,
                                vmem_limit_bytes=VMEM_LIMIT_BYTES)


def _resident_rows(bm, k):
    return pl.BlockSpec((bm, k), lambda i, j: (i, 0), pipeline_mode=pl.Buffered(1))


def _w_cols(w, layer, bn, first_block=0):
    k = w.shape[1]
    return pl.BlockSpec((None, k, bn), lambda i, j: (layer, 0, first_block + j))


def _dot_row_chunks(x_ref, w, emit):
    chunk = min(ROW_CHUNK, x_ref.shape[0])
    for r in range(0, x_ref.shape[0], chunk):
        rows = slice(r, r + chunk)
        emit(rows, jnp.dot(x_ref[rows, :], w, preferred_element_type=F32))


def _mm_kernel(x_ref, w_ref, o_ref):
    def emit(rows, acc):
        o_ref[rows, :] = acc.astype(o_ref.dtype)
    _dot_row_chunks(x_ref, w_ref[...].astype(BF16), emit)


def _mm_residual_kernel(x_ref, w_ref, r_ref, o_ref, *, scale):
    def emit(rows, acc):
        o_ref[rows, :] = scale * r_ref[rows, :] + acc
    _dot_row_chunks(x_ref, w_ref[...].astype(BF16), emit)


def _mm_ln_residual_kernel(x_ref, w_ref, y_ref, mu_ref, rstd_ref, g_ref, b_ref, o_ref):
    reps = y_ref.shape[1] // LANE

    def emit(rows, acc):
        mu = jnp.tile(mu_ref[rows, :], (1, reps))
        rstd = jnp.tile(rstd_ref[rows, :], (1, reps))
        r = (y_ref[rows, :] - mu) * rstd * g_ref[...] + b_ref[...]
        o_ref[rows, :] = ALPHA * r + acc
    _dot_row_chunks(x_ref, w_ref[...].astype(BF16), emit)


def _matmul(x, w, layer, *, bm, bn, out_dtype, resident, name, residual=None,
            ln_residual=None, partial=None, k_split=(0, 1)):
    m = x.shape[0]
    n = w.shape[2]
    k_slice, k_parts = k_split
    k = x.shape[1] // k_parts
    assert k * k_parts == x.shape[1]
    x_spec = pl.BlockSpec((bm, k), lambda i, j: (i, k_slice),
                          **({"pipeline_mode": pl.Buffered(1)} if resident else {}))
    w_spec = pl.BlockSpec((None, k, bn), lambda i, j: (layer, k_slice, j))
    tile = pl.BlockSpec((bm, bn), lambda i, j: (i, j))
    in_specs, args, body = [x_spec, w_spec], [x, w], _mm_kernel
    if residual is not None:
        in_specs, args = in_specs + [tile], args + [residual]
        body = functools.partial(_mm_residual_kernel, scale=ALPHA)
    if partial is not None:
        in_specs, args = in_specs + [tile], args + [partial]
        body = functools.partial(_mm_residual_kernel, scale=1.0)
    if ln_residual is not None:
        stat = pl.BlockSpec((bm, LANE), lambda i, j: (i, 0))
        vec = pl.BlockSpec((1, bn), lambda i, j: (0, j))
        in_specs = in_specs + [tile, stat, stat, vec, vec]
        args, body = args + list(ln_residual), _mm_ln_residual_kernel
    return pl.pallas_call(
        body,
        grid=(m // bm, n // bn),
        in_specs=in_specs,
        out_specs=tile,
        out_shape=jax.ShapeDtypeStruct((m, n), out_dtype),
        compiler_params=_params("parallel", "arbitrary"),
        name=name,
    )(*args)


def _swiglu_kernel(x_ref, wg_ref, wu_ref, wd_ref, o_ref, wdb_ref):
    wg = wg_ref[...].astype(BF16)
    wu = wu_ref[...].astype(BF16)
    for r in range(0, x_ref.shape[0], ROW_CHUNK):
        x = x_ref[r:r + ROW_CHUNK, :]
        gate = jnp.dot(x, wg, preferred_element_type=F32)
        up = jnp.dot(x, wu, preferred_element_type=F32)
        o_ref[r:r + ROW_CHUNK, :] = (gate * jax.nn.sigmoid(gate) * up).astype(o_ref.dtype)
    wdb_ref[...] = wd_ref[...].astype(BF16)


def _swiglu_up(x, w_gate_up, w_down, layer, *, bm, bn, name):
    m, k = x.shape
    _, hidden, d_out = w_down.shape
    ni, nj = m // bm, hidden // bn
    slab = hidden // (ni * nj)
    assert slab * ni * nj == hidden and slab % BF16_SUBLANE == 0
    return pl.pallas_call(
        _swiglu_kernel,
        grid=(ni, nj),
        in_specs=[pl.BlockSpec((bm, k), lambda i, j: (i, 0)),
                  _w_cols(w_gate_up, layer, bn),
                  _w_cols(w_gate_up, layer, bn, first_block=nj),
                  pl.BlockSpec((None, slab, d_out), lambda i, j: (layer, i * nj + j, 0))],
        out_specs=[pl.BlockSpec((bm, bn), lambda i, j: (i, j)),
                   pl.BlockSpec((None, slab, d_out), lambda i, j: (0, i * nj + j, 0))],
        out_shape=[jax.ShapeDtypeStruct((m, hidden), BF16),
                   jax.ShapeDtypeStruct((1, hidden, d_out), BF16)],
        compiler_params=_params("parallel", "arbitrary"),
        name=name,
    )(x, w_gate_up, w_gate_up, w_down)


def _conv_mixer_kernel(x_ref, wb_ref, wc_ref, wh_ref, cw_ref, o_ref, tail_ref, *,
                       tiles_per_seq):
    i, j = pl.program_id(0), pl.program_id(1)
    halo = tail_ref.shape[1]

    @pl.when(i == 0)
    def _():
        tail_ref[j] = jnp.zeros(tail_ref.shape[1:], F32)

    wb = wb_ref[...].astype(BF16)
    wc = wc_ref[...].astype(BF16)
    wh = wh_ref[...].astype(BF16)
    w0, w1, w2 = cw_ref[0:1, :], cw_ref[1:2, :], cw_ref[2:3, :]

    def gated(u, u1, u2, b):
        return (b * (u2 * w0 + u1 * w1 + u * w2)).astype(BF16)

    tail = jnp.where((i % tiles_per_seq) != 0, tail_ref[j], 0.0)
    for r in range(0, x_ref.shape[0], ROW_CHUNK):
        x = x_ref[r:r + ROW_CHUNK, :]
        b = jnp.dot(x, wb, preferred_element_type=F32)
        c = jnp.dot(x, wc, preferred_element_type=F32)
        h = jnp.dot(x, wh, preferred_element_type=F32)
        u = c * h
        o_ref[r:r + ROW_CHUNK, :] = gated(u, pltpu.roll(u, 1, axis=0),
                                          pltpu.roll(u, 2, axis=0), b)
        ext = jnp.concatenate([tail, u[0:halo]], axis=0)
        o_ref[r:r + halo, :] = gated(ext[halo:], pltpu.roll(ext, 1, axis=0)[halo:],
                                     pltpu.roll(ext, 2, axis=0)[halo:], b[0:halo])
        tail = u[ROW_CHUNK - halo:]
    tail_ref[j] = tail


def _conv_mixer(x, w_in, conv_w, layer, *, seq, bm, bn, name):
    m, k = x.shape
    ch = w_in.shape[2] // 3
    taps = conv_w.shape[1]
    nj = ch // bn
    assert seq % bm == 0 and bm % ROW_CHUNK == 0
    kernel = functools.partial(_conv_mixer_kernel, tiles_per_seq=seq // bm)
    return pl.pallas_call(
        kernel,
        grid=(m // bm, nj),
        in_specs=[_resident_rows(bm, k),
                  _w_cols(w_in, layer, bn),
                  _w_cols(w_in, layer, bn, first_block=nj),
                  _w_cols(w_in, layer, bn, first_block=2 * nj),
                  pl.BlockSpec((None, taps, bn), lambda i, j: (layer, 0, j))],
        out_specs=pl.BlockSpec((bm, bn), lambda i, j: (i, j)),
        out_shape=jax.ShapeDtypeStruct((m, ch), BF16),
        scratch_shapes=[pltpu.VMEM((nj, BF16_SUBLANE, bn), F32)],
        compiler_params=_params("arbitrary", "arbitrary"),
        name=name,
    )(x, w_in, w_in, w_in, conv_w)


def _ln_kernel(y_ref, g_ref, b_ref, o_ref, *stat_refs):
    y = y_ref[...]
    mu = jnp.mean(y, axis=-1, keepdims=True)
    yc = y - mu
    var = jnp.mean(yc * yc, axis=-1, keepdims=True)
    rstd = lax.rsqrt(var + LN_EPS)
    o_ref[...] = (yc * rstd * g_ref[...] + b_ref[...]).astype(o_ref.dtype)
    if stat_refs:
        mu_ref, rstd_ref = stat_refs
        mu_ref[...] = jnp.broadcast_to(mu, mu_ref.shape)
        rstd_ref[...] = jnp.broadcast_to(rstd, rstd_ref.shape)


def _layer_norm(y, g, b, *, out_dtype, with_stats, tm, name):
    m, d = y.shape
    row = pl.BlockSpec((tm, d), lambda i: (i, 0))
    vec = pl.BlockSpec((1, d), lambda i: (0, 0))
    stat = pl.BlockSpec((tm, LANE), lambda i: (i, 0))
    out_specs = [row] + [stat, stat] * with_stats
    out_shape = ([jax.ShapeDtypeStruct((m, d), out_dtype)]
                 + [jax.ShapeDtypeStruct((m, LANE), F32)] * (2 * with_stats))
    return pl.pallas_call(
        _ln_kernel,
        grid=(m // tm,),
        in_specs=[row, vec, vec],
        out_specs=out_specs,
        out_shape=out_shape,
        compiler_params=_params("parallel"),
        name=name,
    )(y, g, b)


def _attn_kernel(sinks_ref, q_ref, kc_ref, kp_ref, vc_ref, vp_ref, o_ref, *,
                 n_kv_heads, blocks_per_seq):
    i = pl.program_id(0)
    has_prev = (i % blocks_per_seq) != 0
    nq = GROUP * BLOCK
    kk = lax.broadcasted_iota(jnp.int32, (2 * BLOCK, nq), 0)
    qi = lax.broadcasted_iota(jnp.int32, (2 * BLOCK, nq), 1) % BLOCK
    valid_cur = (kk >= BLOCK) & (kk - BLOCK <= qi)
    valid_prev = (kk < BLOCK) & (kk > qi) & has_prev
    bias = jnp.where(valid_cur | valid_prev, 0.0, -jnp.inf).astype(F32)
    lane_group = lax.broadcasted_iota(jnp.int32, (1, nq), 1) // BLOCK
    scale2 = LOG2_E / math.sqrt(HEAD_DIM)

    for h in range(n_kv_heads):
        ks = slice(h * HEAD_DIM, (h + 1) * HEAD_DIM)
        q = jnp.concatenate(
            [q_ref[:, (h * GROUP + g) * HEAD_DIM:(h * GROUP + g + 1) * HEAD_DIM]
             for g in range(GROUP)], axis=0)
        k = jnp.concatenate([kp_ref[:, ks], kc_ref[:, ks]], axis=0)
        v = jnp.concatenate([vp_ref[:, ks], vc_ref[:, ks]], axis=0)
        st = lax.dot_general(k, q, (((1,), (1,)), ((), ())),
                             preferred_element_type=F32) * scale2 + bias
        sink = jnp.zeros((1, nq), F32)
        for g in range(GROUP):
            sink = jnp.where(lane_group == g, sinks_ref[h * GROUP + g] * LOG2_E, sink)
        m = jnp.maximum(jnp.max(st, axis=0, keepdims=True), sink)
        p = jnp.exp2(st - m)
        denom = jnp.sum(p, axis=0, keepdims=True) + jnp.exp2(sink - m)
        probs = (p / denom).astype(BF16)
        o = lax.dot_general(probs, v, (((0,), (0,)), ((), ())),
                            preferred_element_type=F32)
        for g in range(GROUP):
            col = (h * GROUP + g) * HEAD_DIM
            o_ref[:, col:col + HEAD_DIM] = o[g * BLOCK:(g + 1) * BLOCK].astype(BF16)


def _attention(qkv, sinks, *, seq, n_heads, name):
    m = qkv.shape[0]
    n_kv_heads = n_heads // GROUP
    q_width = n_heads * HEAD_DIM
    kv_width = n_kv_heads * HEAD_DIM
    k_col = q_width // kv_width
    v_col = k_col + 1
    prev = lambda i: jnp.maximum(i - 1, 0)
    kernel = functools.partial(_attn_kernel, n_kv_heads=n_kv_heads,
                               blocks_per_seq=seq // BLOCK)
    return pl.pallas_call(
        kernel,
        grid=(m // BLOCK,),
        in_specs=[pl.BlockSpec(memory_space=pltpu.SMEM),
                  pl.BlockSpec((BLOCK, q_width), lambda i: (i, 0)),
                  pl.BlockSpec((BLOCK, kv_width), lambda i: (i, k_col)),
                  pl.BlockSpec((BLOCK, kv_width), lambda i: (prev(i), k_col)),
                  pl.BlockSpec((BLOCK, kv_width), lambda i: (i, v_col)),
                  pl.BlockSpec((BLOCK, kv_width), lambda i: (prev(i), v_col))],
        out_specs=pl.BlockSpec((BLOCK, q_width), lambda i: (i, 0)),
        out_shape=jax.ShapeDtypeStruct((m, q_width), BF16),
        compiler_params=_params("parallel"),
        name=name,
    )(sinks, qkv, qkv, qkv, qkv, qkv)


def _mm_d(x, w, layer, out_dtype, name, **residual):
    return _matmul(x, w, layer, bm=MM_BM, bn=MM_BN, out_dtype=out_dtype,
                   resident=bool(residual), name=name, **residual)


def _ffn(xb, w_gate_up, w_down, layer, ln_residual, *, name):
    act, w_down_bf16 = _swiglu_up(xb, w_gate_up, w_down, layer, bm=MM_BM, bn=UP_BN,
                                  name=name + "_up")
    down = functools.partial(_matmul, act, w_down_bf16, 0, bm=DOWN_BM, bn=DOWN_BN,
                             out_dtype=F32, resident=False)
    part = down(ln_residual=ln_residual, k_split=(0, 2), name=name + "_down_a")
    return down(partial=part, k_split=(1, 2), name=name + "_down_b")


def kernel(x, attn_w_in, attn_sinks, attn_w_out, conv_w_in, conv_w, conv_w_out,
           ln_mix_g, ln_mix_b, ffn_w_gate_up, ffn_w_down, ln_ffn_g, ln_ffn_b):
    batch, seq, d = x.shape
    m = batch * seq
    n_heads = attn_sinks.shape[1]
    x = x.reshape(m, d)
    xb = x.astype(BF16)
    vec = lambda p, i: p[i].reshape(1, d)

    def norm(y, g, b, name):
        xb, mu, rstd = _layer_norm(y, g, b, out_dtype=BF16, with_stats=True, tm=LN_TM,
                                   name=name)
        return xb, (y, mu, rstd, g, b)


    qkv = _mm_d(xb, attn_w_in, 0, BF16, "attn_qkv")
    attn = _attention(qkv, attn_sinks[0], seq=seq, n_heads=n_heads, name="attn_core")
    y = _mm_d(attn, attn_w_out, 0, F32, "attn_out", residual=x)
    xb, res = norm(y, vec(ln_mix_g, 0), vec(ln_mix_b, 0), "attn_ln")
    y = _ffn(xb, ffn_w_gate_up, ffn_w_down, 0, res, name="ffn0")
    xb, res = norm(y, vec(ln_ffn_g, 0), vec(ln_ffn_b, 0), "ffn0_ln")

    gated = _conv_mixer(xb, conv_w_in, conv_w, 0, seq=seq, bm=CONV_IN_BM, bn=UP_BN,
                        name="conv_mixer")
    y = _mm_d(gated, conv_w_out, 0, F32, "conv_out", ln_residual=res)
    xb, res = norm(y, vec(ln_mix_g, 1), vec(ln_mix_b, 1), "conv_ln")
    y = _ffn(xb, ffn_w_gate_up, ffn_w_down, 1, res, name="ffn1")
    (out,) = _layer_norm(y, vec(ln_ffn_g, 1), vec(ln_ffn_b, 1), out_dtype=F32,
                         with_stats=False, tm=LN_TM, name="ffn1_ln")
    return out.reshape(batch, seq, d)
```

```python
import functools
import math

import jax
import jax.numpy as jnp
from jax import lax
from jax.experimental import pallas as pl
from jax.experimental.pallas import tpu as pltpu

DEPTH = 2
HEAD_DIM = 128
GROUP = 4
BLOCK = 128
CONV_WIDTH = 3
ALPHA = (2.0 * DEPTH) ** 0.25
LN_EPS = 1e-5
LOG2_E = math.log2(math.e)

BF16_SUBLANE = 16
VMEM_LIMIT_BYTES = 58 * 1024 * 1024

MM_BM = 2048
MM_BN = 512
UP_BN = 256
CONV_IN_BM = 2048
ROW_CHUNK = 512
QKV_BM = 1024
DOWN_BM = 512
DOWN_BN = 512
LN_TM = 512
LANE = 128

F32 = jnp.float32
BF16 = jnp.bfloat16


def _params(*semantics):
    return pltpu.CompilerParams(dimension_semantics=semantics,
                                vmem_limit_bytes=VMEM_LIMIT_BYTES)


def _resident_rows(bm, k):
    return pl.BlockSpec((bm, k), lambda i, j: (i, 0), pipeline_mode=pl.Buffered(1))


def _w_cols(w, layer, bn, first_block=0):
    k = w.shape[1]
    return pl.BlockSpec((None, k, bn), lambda i, j: (layer, 0, first_block + j))


def _dot_row_chunks(x_ref, w, emit):
    chunk = min(ROW_CHUNK, x_ref.shape[0])
    for r in range(0, x_ref.shape[0], chunk):
        rows = slice(r, r + chunk)
        emit(rows, jnp.dot(x_ref[rows, :].astype(BF16), w, preferred_element_type=F32))


def _mm_kernel(x_ref, w_ref, o_ref):
    def emit(rows, acc):
        o_ref[rows, :] = acc.astype(o_ref.dtype)
    _dot_row_chunks(x_ref, w_ref[...].astype(BF16), emit)


def _mm_residual_kernel(x_ref, w_ref, r_ref, o_ref):
    def emit(rows, acc):
        o_ref[rows, :] = ALPHA * r_ref[rows, :] + acc
    _dot_row_chunks(x_ref, w_ref[...].astype(BF16), emit)


def _mm_ln_residual_kernel(x_ref, w_ref, y_ref, mu_ref, rstd_ref, g_ref, b_ref, o_ref):
    reps = y_ref.shape[1] // LANE

    def emit(rows, acc):
        mu = jnp.tile(mu_ref[rows, :], (1, reps))
        rstd = jnp.tile(rstd_ref[rows, :], (1, reps))
        r = (y_ref[rows, :] - mu) * rstd * g_ref[...] + b_ref[...]
        o_ref[rows, :] = ALPHA * r + acc
    _dot_row_chunks(x_ref, w_ref[...].astype(BF16), emit)


def _matmul(x, w, layer, *, bm, bn, out_dtype, resident, name, residual=None,
            ln_residual=None):
    m, k = x.shape
    n = w.shape[2]
    x_spec = (_resident_rows(bm, k) if resident
              else pl.BlockSpec((bm, k), lambda i, j: (i, 0)))
    tile = pl.BlockSpec((bm, bn), lambda i, j: (i, j))
    in_specs, args, body = [x_spec, _w_cols(w, layer, bn)], [x, w], _mm_kernel
    if residual is not None:
        in_specs, args, body = in_specs + [tile], args + [residual], _mm_residual_kernel
    if ln_residual is not None:
        stat = pl.BlockSpec((bm, LANE), lambda i, j: (i, 0))
        vec = pl.BlockSpec((1, bn), lambda i, j: (0, j))
        in_specs = in_specs + [tile, stat, stat, vec, vec]
        args, body = args + list(ln_residual), _mm_ln_residual_kernel
    return pl.pallas_call(
        body,
        grid=(m // bm, n // bn),
        in_specs=in_specs,
        out_specs=tile,
        out_shape=jax.ShapeDtypeStruct((m, n), out_dtype),
        compiler_params=_params("parallel", "arbitrary"),
        name=name,
    )(*args)


def _swiglu_kernel(x_ref, wg_ref, wu_ref, wd_ref, o_ref, wdb_ref):
    wg = wg_ref[...].astype(BF16)
    wu = wu_ref[...].astype(BF16)
    for r in range(0, x_ref.shape[0], ROW_CHUNK):
        x = x_ref[r:r + ROW_CHUNK, :]
        gate = jnp.dot(x, wg, preferred_element_type=F32)
        up = jnp.dot(x, wu, preferred_element_type=F32)
        o_ref[r:r + ROW_CHUNK, :] = (gate * jax.nn.sigmoid(gate) * up).astype(o_ref.dtype)
    wdb_ref[...] = wd_ref[...].astype(BF16)


def _swiglu_up(x, w_gate_up, w_down, layer, *, bm, bn, name):
    m, k = x.shape
    _, hidden, d_out = w_down.shape
    ni, nj = m // bm, hidden // bn
    slab = hidden // (ni * nj)
    assert slab * ni * nj == hidden and slab % BF16_SUBLANE == 0
    return pl.pallas_call(
        _swiglu_kernel,
        grid=(ni, nj),
        in_specs=[pl.BlockSpec((bm, k), lambda i, j: (i, 0)),
                  _w_cols(w_gate_up, layer, bn),
                  _w_cols(w_gate_up, layer, bn, first_block=nj),
                  pl.BlockSpec((None, slab, d_out), lambda i, j: (layer, i * nj + j, 0))],
        out_specs=[pl.BlockSpec((bm, bn), lambda i, j: (i, j)),
                   pl.BlockSpec((None, slab, d_out), lambda i, j: (0, i * nj + j, 0))],
        out_shape=[jax.ShapeDtypeStruct((m, hidden), BF16),
                   jax.ShapeDtypeStruct((1, hidden, d_out), BF16)],
        compiler_params=_params("parallel", "arbitrary"),
        name=name,
    )(x, w_gate_up, w_gate_up, w_down)


def _conv_mixer_kernel(x_ref, wb_ref, wc_ref, wh_ref, cw_ref, o_ref, tail_ref, *,
                       tiles_per_seq):
    i, j = pl.program_id(0), pl.program_id(1)
    halo = tail_ref.shape[1]

    @pl.when(i == 0)
    def _():
        tail_ref[j] = jnp.zeros(tail_ref.shape[1:], F32)

    wb = wb_ref[...].astype(BF16)
    wc = wc_ref[...].astype(BF16)
    wh = wh_ref[...].astype(BF16)
    w0, w1, w2 = cw_ref[0:1, :], cw_ref[1:2, :], cw_ref[2:3, :]

    def gated(u, u1, u2, b):
        return (b * (u2 * w0 + u1 * w1 + u * w2)).astype(BF16)

    tail = jnp.where((i % tiles_per_seq) != 0, tail_ref[j], 0.0)
    for r in range(0, x_ref.shape[0], ROW_CHUNK):
        x = x_ref[r:r + ROW_CHUNK, :]
        b = jnp.dot(x, wb, preferred_element_type=F32)
        c = jnp.dot(x, wc, preferred_element_type=F32)
        h = jnp.dot(x, wh, preferred_element_type=F32)
        u = c * h
        o_ref[r:r + ROW_CHUNK, :] = gated(u, pltpu.roll(u, 1, axis=0),
                                          pltpu.roll(u, 2, axis=0), b)
        ext = jnp.concatenate([tail, u[0:halo]], axis=0)
        o_ref[r:r + halo, :] = gated(ext[halo:], pltpu.roll(ext, 1, axis=0)[halo:],
                                     pltpu.roll(ext, 2, axis=0)[halo:], b[0:halo])
        tail = u[ROW_CHUNK - halo:]
    tail_ref[j] = tail


def _conv_mixer(x, w_in, conv_w, layer, *, seq, bm, bn, name):
    m, k = x.shape
    ch = w_in.shape[2] // 3
    taps = conv_w.shape[1]
    nj = ch // bn
    assert seq % bm == 0 and bm % ROW_CHUNK == 0
    kernel = functools.partial(_conv_mixer_kernel, tiles_per_seq=seq // bm)
    return pl.pallas_call(
        kernel,
        grid=(m // bm, nj),
        in_specs=[_resident_rows(bm, k),
                  _w_cols(w_in, layer, bn),
                  _w_cols(w_in, layer, bn, first_block=nj),
                  _w_cols(w_in, layer, bn, first_block=2 * nj),
                  pl.BlockSpec((None, taps, bn), lambda i, j: (layer, 0, j))],
        out_specs=pl.BlockSpec((bm, bn), lambda i, j: (i, j)),
        out_shape=jax.ShapeDtypeStruct((m, ch), BF16),
        scratch_shapes=[pltpu.VMEM((nj, BF16_SUBLANE, bn), F32)],
        compiler_params=_params("arbitrary", "arbitrary"),
        name=name,
    )(x, w_in, w_in, w_in, conv_w)


def _ln_kernel(y_ref, g_ref, b_ref, o_ref, *stat_refs):
    y = y_ref[...]
    mu = jnp.mean(y, axis=-1, keepdims=True)
    yc = y - mu
    var = jnp.mean(yc * yc, axis=-1, keepdims=True)
    rstd = lax.rsqrt(var + LN_EPS)
    o_ref[...] = (yc * rstd * g_ref[...] + b_ref[...]).astype(o_ref.dtype)
    if stat_refs:
        mu_ref, rstd_ref = stat_refs
        mu_ref[...] = jnp.broadcast_to(mu, mu_ref.shape)
        rstd_ref[...] = jnp.broadcast_to(rstd, rstd_ref.shape)


def _layer_norm(y, g, b, *, out_dtype, with_stats, tm, name):
    m, d = y.shape
    row = pl.BlockSpec((tm, d), lambda i: (i, 0))
    vec = pl.BlockSpec((1, d), lambda i: (0, 0))
    stat = pl.BlockSpec((tm, LANE), lambda i: (i, 0))
    out_specs = [row] + [stat, stat] * with_stats
    out_shape = ([jax.ShapeDtypeStruct((m, d), out_dtype)]
                 + [jax.ShapeDtypeStruct((m, LANE), F32)] * (2 * with_stats))
    return pl.pallas_call(
        _ln_kernel,
        grid=(m // tm,),
        in_specs=[row, vec, vec],
        out_specs=out_specs,
        out_shape=out_shape,
        compiler_params=_params("parallel"),
        name=name,
    )(y, g, b)


def _attn_kernel(sinks_ref, q_ref, kc_ref, kp_ref, vc_ref, vp_ref, o_ref, *,
                 n_kv_heads, blocks_per_seq):
    i = pl.program_id(0)
    has_prev = (i % blocks_per_seq) != 0
    nq = GROUP * BLOCK
    kk = lax.broadcasted_iota(jnp.int32, (2 * BLOCK, nq), 0)
    qi = lax.broadcasted_iota(jnp.int32, (2 * BLOCK, nq), 1) % BLOCK
    valid_cur = (kk >= BLOCK) & (kk - BLOCK <= qi)
    valid_prev = (kk < BLOCK) & (kk > qi) & has_prev
    bias = jnp.where(valid_cur | valid_prev, 0.0, -jnp.inf).astype(F32)
    lane_group = lax.broadcasted_iota(jnp.int32, (1, nq), 1) // BLOCK
    scale2 = LOG2_E / math.sqrt(HEAD_DIM)

    for h in range(n_kv_heads):
        ks = slice(h * HEAD_DIM, (h + 1) * HEAD_DIM)
        q = jnp.concatenate(
            [q_ref[:, (h * GROUP + g) * HEAD_DIM:(h * GROUP + g + 1) * HEAD_DIM]
             for g in range(GROUP)], axis=0)
        k = jnp.concatenate([kp_ref[:, ks], kc_ref[:, ks]], axis=0)
        v = jnp.concatenate([vp_ref[:, ks], vc_ref[:, ks]], axis=0)
        st = lax.dot_general(k, q, (((1,), (1,)), ((), ())),
                             preferred_element_type=F32) * scale2 + bias
        sink = jnp.zeros((1, nq), F32)
        for g in range(GROUP):
            sink = jnp.where(lane_group == g, sinks_ref[h * GROUP + g] * LOG2_E, sink)
        m = jnp.maximum(jnp.max(st, axis=0, keepdims=True), sink)
        p = jnp.exp2(st - m)
        denom = jnp.sum(p, axis=0, keepdims=True) + jnp.exp2(sink - m)
        probs = (p / denom).astype(BF16)
        o = lax.dot_general(probs, v, (((0,), (0,)), ((), ())),
                            preferred_element_type=F32)
        for g in range(GROUP):
            col = (h * GROUP + g) * HEAD_DIM
            o_ref[:, col:col + HEAD_DIM] = o[g * BLOCK:(g + 1) * BLOCK].astype(BF16)


def _attention(qkv, sinks, *, seq, n_heads, name):
    m = qkv.shape[0]
    n_kv_heads = n_heads // GROUP
    q_width = n_heads * HEAD_DIM
    kv_width = n_kv_heads * HEAD_DIM
    k_col = q_width // kv_width
    v_col = k_col + 1
    prev = lambda i: jnp.maximum(i - 1, 0)
    kernel = functools.partial(_attn_kernel, n_kv_heads=n_kv_heads,
                               blocks_per_seq=seq // BLOCK)
    return pl.pallas_call(
        kernel,
        grid=(m // BLOCK,),
        in_specs=[pl.BlockSpec(memory_space=pltpu.SMEM),
                  pl.BlockSpec((BLOCK, q_width), lambda i: (i, 0)),
                  pl.BlockSpec((BLOCK, kv_width), lambda i: (i, k_col)),
                  pl.BlockSpec((BLOCK, kv_width), lambda i: (prev(i), k_col)),
                  pl.BlockSpec((BLOCK, kv_width), lambda i: (i, v_col)),
                  pl.BlockSpec((BLOCK, kv_width), lambda i: (prev(i), v_col))],
        out_specs=pl.BlockSpec((BLOCK, q_width), lambda i: (i, 0)),
        out_shape=jax.ShapeDtypeStruct((m, q_width), BF16),
        compiler_params=_params("parallel"),
        name=name,
    )(sinks, qkv, qkv, qkv, qkv, qkv)


def _mm_d(x, w, layer, out_dtype, name, **residual):
    return _matmul(x, w, layer, bm=MM_BM, bn=MM_BN, out_dtype=out_dtype,
                   resident=bool(residual), name=name, **residual)


def _ffn(xb, w_gate_up, w_down, layer, ln_residual, *, name):
    act, w_down_bf16 = _swiglu_up(xb, w_gate_up, w_down, layer, bm=MM_BM, bn=UP_BN,
                                  name=name + "_up")
    return _matmul(act, w_down_bf16, 0, bm=DOWN_BM, bn=DOWN_BN, out_dtype=F32,
                   resident=False, ln_residual=ln_residual, name=name + "_down")


def kernel(x, attn_w_in, attn_sinks, attn_w_out, conv_w_in, conv_w, conv_w_out,
           ln_mix_g, ln_mix_b, ffn_w_gate_up, ffn_w_down, ln_ffn_g, ln_ffn_b):
    batch, seq, d = x.shape
    m = batch * seq
    n_heads = attn_sinks.shape[1]
    x = x.reshape(m, d)
    vec = lambda p, i: p[i].reshape(1, d)

    def norm(y, g, b, name):
        xb, mu, rstd = _layer_norm(y, g, b, out_dtype=BF16, with_stats=True, tm=LN_TM,
                                   name=name)
        return xb, (y, mu, rstd, g, b)


    qkv = _matmul(x, attn_w_in, 0, bm=QKV_BM, bn=MM_BN, out_dtype=BF16, resident=False,
                  name="attn_qkv")
    attn = _attention(qkv, attn_sinks[0], seq=seq, n_heads=n_heads, name="attn_core")
    y = _mm_d(attn, attn_w_out, 0, F32, "attn_out", residual=x)
    xb, res = norm(y, vec(ln_mix_g, 0), vec(ln_mix_b, 0), "attn_ln")
    y = _ffn(xb, ffn_w_gate_up, ffn_w_down, 0, res, name="ffn0")
    xb, res = norm(y, vec(ln_ffn_g, 0), vec(ln_ffn_b, 0), "ffn0_ln")

    gated = _conv_mixer(xb, conv_w_in, conv_w, 0, seq=seq, bm=CONV_IN_BM, bn=UP_BN,
                        name="conv_mixer")
    y = _mm_d(gated, conv_w_out, 0, F32, "conv_out", ln_residual=res)
    xb, res = norm(y, vec(ln_mix_g, 1), vec(ln_mix_b, 1), "conv_ln")
    y = _ffn(xb, ffn_w_gate_up, ffn_w_down, 1, res, name="ffn1")
    (out,) = _layer_norm(y, vec(ln_ffn_g, 1), vec(ln_ffn_b, 1), out_dtype=F32,
                         with_stats=False, tm=LN_TM, name="ffn1_ln")
    return out.reshape(batch, seq, d)
```

```python
import functools
import math

import jax
import jax.numpy as jnp
from jax import lax
from jax.experimental import pallas as pl
from jax.experimental.pallas import tpu as pltpu

DEPTH = 2
HEAD_DIM = 128
GROUP = 4
BLOCK = 128
CONV_WIDTH = 3
ALPHA = (2.0 * DEPTH) ** 0.25
LN_EPS = 1e-5
LOG2_E = math.log2(math.e)

BF16_SUBLANE = 16
VMEM_LIMIT_BYTES = 58 * 1024 * 1024

MM_BM = 2048
MM_BN = 512
UP_BN = 256
CONV_IN_BM = 2048
ROW_CHUNK = 512
QKV_BM = 1024
DOWN_BM = 512
DOWN_BN = 512
LN_TM = 512
LANE = 128

F32 = jnp.float32
BF16 = jnp.bfloat16


def _params(*semantics):
    return pltpu.CompilerParams(dimension_semantics=semantics,
                                vmem_limit_bytes=VMEM_LIMIT_BYTES)


def _resident_rows(bm, k):
    return pl.BlockSpec((bm, k), lambda i, j: (i, 0), pipeline_mode=pl.Buffered(1))


def _w_cols(w, layer, bn, first_block=0):
    k = w.shape[1]
    return pl.BlockSpec((None, k, bn), lambda i, j: (layer, 0, first_block + j))


def _dot_row_chunks(x_ref, w, emit):
    chunk = min(ROW_CHUNK, x_ref.shape[0])
    for r in range(0, x_ref.shape[0], chunk):
        rows = slice(r, r + chunk)
        emit(rows, jnp.dot(x_ref[rows, :].astype(BF16), w, preferred_element_type=F32))


def _mm_kernel(x_ref, w_ref, o_ref):
    def emit(rows, acc):
        o_ref[rows, :] = acc.astype(o_ref.dtype)
    _dot_row_chunks(x_ref, w_ref[...].astype(BF16), emit)


def _mm_residual_kernel(x_ref, w_ref, r_ref, o_ref):
    def emit(rows, acc):
        o_ref[rows, :] = ALPHA * r_ref[rows, :] + acc
    _dot_row_chunks(x_ref, w_ref[...].astype(BF16), emit)


def _mm_ln_residual_kernel(x_ref, w_ref, y_ref, mu_ref, rstd_ref, g_ref, b_ref, o_ref):
    reps = y_ref.shape[1] // LANE
    j = pl.program_id(1)
    g, b = g_ref[j], b_ref[j]

    def emit(rows, acc):
        mu = jnp.tile(mu_ref[rows, :], (1, reps))
        rstd = jnp.tile(rstd_ref[rows, :], (1, reps))
        r = (y_ref[rows, :] - mu) * rstd * g + b
        o_ref[rows, :] = ALPHA * r + acc
    _dot_row_chunks(x_ref, w_ref[...].astype(BF16), emit)


def _matmul(x, w, layer, *, bm, bn, out_dtype, resident, name, residual=None,
            ln_residual=None):
    m, k = x.shape
    n = w.shape[2]
    x_spec = (_resident_rows(bm, k) if resident
              else pl.BlockSpec((bm, k), lambda i, j: (i, 0)))
    tile = pl.BlockSpec((bm, bn), lambda i, j: (i, j))
    in_specs, args, body = [x_spec, _w_cols(w, layer, bn)], [x, w], _mm_kernel
    if residual is not None:
        in_specs, args, body = in_specs + [tile], args + [residual], _mm_residual_kernel
    if ln_residual is not None:
        y, mu, rstd, g, b = ln_residual
        stat = pl.BlockSpec((bm, LANE), lambda i, j: (i, 0))
        vec = pl.BlockSpec((n // bn, 1, bn), lambda i, j: (0, 0, 0))
        in_specs = in_specs + [tile, stat, stat, vec, vec]
        args = args + [y, mu, rstd, g.reshape(vec.block_shape), b.reshape(vec.block_shape)]
        body = _mm_ln_residual_kernel
    return pl.pallas_call(
        body,
        grid=(m // bm, n // bn),
        in_specs=in_specs,
        out_specs=tile,
        out_shape=jax.ShapeDtypeStruct((m, n), out_dtype),
        compiler_params=_params("parallel", "arbitrary"),
        name=name,
    )(*args)


def _swiglu_kernel(x_ref, wg_ref, wu_ref, wd_ref, o_ref, wdb_ref):
    wg = wg_ref[...].astype(BF16)
    wu = wu_ref[...].astype(BF16)
    for r in range(0, x_ref.shape[0], ROW_CHUNK):
        x = x_ref[r:r + ROW_CHUNK, :]
        gate = jnp.dot(x, wg, preferred_element_type=F32)
        up = jnp.dot(x, wu, preferred_element_type=F32)
        o_ref[r:r + ROW_CHUNK, :] = (gate * jax.nn.sigmoid(gate) * up).astype(o_ref.dtype)
    wdb_ref[...] = wd_ref[...].astype(BF16)


def _swiglu_up(x, w_gate_up, w_down, layer, *, bm, bn, name):
    m, k = x.shape
    _, hidden, d_out = w_down.shape
    ni, nj = m // bm, hidden // bn
    slab = hidden // (ni * nj)
    assert slab * ni * nj == hidden and slab % BF16_SUBLANE == 0
    return pl.pallas_call(
        _swiglu_kernel,
        grid=(ni, nj),
        in_specs=[pl.BlockSpec((bm, k), lambda i, j: (i, 0)),
                  _w_cols(w_gate_up, layer, bn),
                  _w_cols(w_gate_up, layer, bn, first_block=nj),
                  pl.BlockSpec((None, slab, d_out), lambda i, j: (layer, i * nj + j, 0))],
        out_specs=[pl.BlockSpec((bm, bn), lambda i, j: (i, j)),
                   pl.BlockSpec((None, slab, d_out), lambda i, j: (0, i * nj + j, 0))],
        out_shape=[jax.ShapeDtypeStruct((m, hidden), BF16),
                   jax.ShapeDtypeStruct((1, hidden, d_out), BF16)],
        compiler_params=_params("parallel", "arbitrary"),
        name=name,
    )(x, w_gate_up, w_gate_up, w_down)


def _conv_mixer_kernel(x_ref, wb_ref, wc_ref, wh_ref, cw_ref, o_ref, tail_ref, *,
                       tiles_per_seq):
    i, j = pl.program_id(0), pl.program_id(1)
    halo = tail_ref.shape[1]

    @pl.when(i == 0)
    def _():
        tail_ref[j] = jnp.zeros(tail_ref.shape[1:], F32)

    wb = wb_ref[...].astype(BF16)
    wc = wc_ref[...].astype(BF16)
    wh = wh_ref[...].astype(BF16)
    cw = cw_ref[j]
    w0, w1, w2 = cw[0:1, :], cw[1:2, :], cw[2:3, :]

    def gated(u, u1, u2, b):
        return (b * (u2 * w0 + u1 * w1 + u * w2)).astype(BF16)

    tail = jnp.where((i % tiles_per_seq) != 0, tail_ref[j], 0.0)
    for r in range(0, x_ref.shape[0], ROW_CHUNK):
        x = x_ref[r:r + ROW_CHUNK, :]
        b = jnp.dot(x, wb, preferred_element_type=F32)
        c = jnp.dot(x, wc, preferred_element_type=F32)
        h = jnp.dot(x, wh, preferred_element_type=F32)
        u = c * h
        o_ref[r:r + ROW_CHUNK, :] = gated(u, pltpu.roll(u, 1, axis=0),
                                          pltpu.roll(u, 2, axis=0), b)
        ext = jnp.concatenate([tail, u[0:halo]], axis=0)
        o_ref[r:r + halo, :] = gated(ext[halo:], pltpu.roll(ext, 1, axis=0)[halo:],
                                     pltpu.roll(ext, 2, axis=0)[halo:], b[0:halo])
        tail = u[ROW_CHUNK - halo:]
    tail_ref[j] = tail


def _conv_mixer(x, w_in, conv_w, layer, *, seq, bm, bn, name):
    m, k = x.shape
    ch = w_in.shape[2] // 3
    taps = conv_w.shape[1]
    nj = ch // bn
    assert seq % bm == 0 and bm % ROW_CHUNK == 0
    tap_tiles = conv_w[layer].reshape(taps, nj, bn).transpose(1, 0, 2)
    kernel = functools.partial(_conv_mixer_kernel, tiles_per_seq=seq // bm)
    return pl.pallas_call(
        kernel,
        grid=(m // bm, nj),
        in_specs=[_resident_rows(bm, k),
                  _w_cols(w_in, layer, bn),
                  _w_cols(w_in, layer, bn, first_block=nj),
                  _w_cols(w_in, layer, bn, first_block=2 * nj),
                  pl.BlockSpec((nj, taps, bn), lambda i, j: (0, 0, 0))],
        out_specs=pl.BlockSpec((bm, bn), lambda i, j: (i, j)),
        out_shape=jax.ShapeDtypeStruct((m, ch), BF16),
        scratch_shapes=[pltpu.VMEM((nj, BF16_SUBLANE, bn), F32)],
        compiler_params=_params("arbitrary", "arbitrary"),
        name=name,
    )(x, w_in, w_in, w_in, tap_tiles)


def _ln_kernel(y_ref, g_ref, b_ref, o_ref, *stat_refs):
    y = y_ref[...]
    mu = jnp.mean(y, axis=-1, keepdims=True)
    yc = y - mu
    var = jnp.mean(yc * yc, axis=-1, keepdims=True)
    rstd = lax.rsqrt(var + LN_EPS)
    o_ref[...] = (yc * rstd * g_ref[...] + b_ref[...]).astype(o_ref.dtype)
    if stat_refs:
        mu_ref, rstd_ref = stat_refs
        mu_ref[...] = jnp.broadcast_to(mu, mu_ref.shape)
        rstd_ref[...] = jnp.broadcast_to(rstd, rstd_ref.shape)


def _layer_norm(y, g, b, *, out_dtype, with_stats, tm, name):
    m, d = y.shape
    row = pl.BlockSpec((tm, d), lambda i: (i, 0))
    vec = pl.BlockSpec((1, d), lambda i: (0, 0))
    stat = pl.BlockSpec((tm, LANE), lambda i: (i, 0))
    out_specs = [row] + [stat, stat] * with_stats
    out_shape = ([jax.ShapeDtypeStruct((m, d), out_dtype)]
                 + [jax.ShapeDtypeStruct((m, LANE), F32)] * (2 * with_stats))
    return pl.pallas_call(
        _ln_kernel,
        grid=(m // tm,),
        in_specs=[row, vec, vec],
        out_specs=out_specs,
        out_shape=out_shape,
        compiler_params=_params("parallel"),
        name=name,
    )(y, g, b)


def _attn_kernel(sinks_ref, q_ref, kc_ref, kp_ref, vc_ref, vp_ref, o_ref, *,
                 n_kv_heads, blocks_per_seq):
    i = pl.program_id(0)
    has_prev = (i % blocks_per_seq) != 0
    nq = GROUP * BLOCK
    kk = lax.broadcasted_iota(jnp.int32, (2 * BLOCK, nq), 0)
    qi = lax.broadcasted_iota(jnp.int32, (2 * BLOCK, nq), 1) % BLOCK
    valid_cur = (kk >= BLOCK) & (kk - BLOCK <= qi)
    valid_prev = (kk < BLOCK) & (kk > qi) & has_prev
    bias = jnp.where(valid_cur | valid_prev, 0.0, -jnp.inf).astype(F32)
    lane_group = lax.broadcasted_iota(jnp.int32, (1, nq), 1) // BLOCK
    scale2 = LOG2_E / math.sqrt(HEAD_DIM)

    for h in range(n_kv_heads):
        ks = slice(h * HEAD_DIM, (h + 1) * HEAD_DIM)
        q = jnp.concatenate(
            [q_ref[:, (h * GROUP + g) * HEAD_DIM:(h * GROUP + g + 1) * HEAD_DIM]
             for g in range(GROUP)], axis=0)
        k = jnp.concatenate([kp_ref[:, ks], kc_ref[:, ks]], axis=0)
        v = jnp.concatenate([vp_ref[:, ks], vc_ref[:, ks]], axis=0)
        st = lax.dot_general(k, q, (((1,), (1,)), ((), ())),
                             preferred_element_type=F32) * scale2 + bias
        sink = jnp.zeros((1, nq), F32)
        for g in range(GROUP):
            sink = jnp.where(lane_group == g, sinks_ref[h * GROUP + g] * LOG2_E, sink)
        m = jnp.maximum(jnp.max(st, axis=0, keepdims=True), sink)
        p = jnp.exp2(st - m)
        denom = jnp.sum(p, axis=0, keepdims=True) + jnp.exp2(sink - m)
        probs = (p / denom).astype(BF16)
        o = lax.dot_general(probs, v, (((0,), (0,)), ((), ())),
                            preferred_element_type=F32)
        for g in range(GROUP):
            col = (h * GROUP + g) * HEAD_DIM
            o_ref[:, col:col + HEAD_DIM] = o[g * BLOCK:(g + 1) * BLOCK].astype(BF16)


def _attention(qkv, sinks, *, seq, n_heads, name):
    m = qkv.shape[0]
    n_kv_heads = n_heads // GROUP
    q_width = n_heads * HEAD_DIM
    kv_width = n_kv_heads * HEAD_DIM
    k_col = q_width // kv_width
    v_col = k_col + 1
    prev = lambda i: jnp.maximum(i - 1, 0)
    kernel = functools.partial(_attn_kernel, n_kv_heads=n_kv_heads,
                               blocks_per_seq=seq // BLOCK)
    return pl.pallas_call(
        kernel,
        grid=(m // BLOCK,),
        in_specs=[pl.BlockSpec(memory_space=pltpu.SMEM),
                  pl.BlockSpec((BLOCK, q_width), lambda i: (i, 0)),
                  pl.BlockSpec((BLOCK, kv_width), lambda i: (i, k_col)),
                  pl.BlockSpec((BLOCK, kv_width), lambda i: (prev(i), k_col)),
                  pl.BlockSpec((BLOCK, kv_width), lambda i: (i, v_col)),
                  pl.BlockSpec((BLOCK, kv_width), lambda i: (prev(i), v_col))],
        out_specs=pl.BlockSpec((BLOCK, q_width), lambda i: (i, 0)),
        out_shape=jax.ShapeDtypeStruct((m, q_width), BF16),
        compiler_params=_params("parallel"),
        name=name,
    )(sinks, qkv, qkv, qkv, qkv, qkv)


def _mm_d(x, w, layer, out_dtype, name, **residual):
    return _matmul(x, w, layer, bm=MM_BM, bn=MM_BN, out_dtype=out_dtype,
                   resident=bool(residual), name=name, **residual)


def _ffn(xb, w_gate_up, w_down, layer, ln_residual, *, name):
    act, w_down_bf16 = _swiglu_up(xb, w_gate_up, w_down, layer, bm=MM_BM, bn=UP_BN,
                                  name=name + "_up")
    return _matmul(act, w_down_bf16, 0, bm=DOWN_BM, bn=DOWN_BN, out_dtype=F32,
                   resident=False, ln_residual=ln_residual, name=name + "_down")


def kernel(x, attn_w_in, attn_sinks, attn_w_out, conv_w_in, conv_w, conv_w_out,
           ln_mix_g, ln_mix_b, ffn_w_gate_up, ffn_w_down, ln_ffn_g, ln_ffn_b):
    batch, seq, d = x.shape
    m = batch * seq
    n_heads = attn_sinks.shape[1]
    x = x.reshape(m, d)
    vec = lambda p, i: p[i].reshape(1, d)

    def norm(y, g, b, name):
        xb, mu, rstd = _layer_norm(y, g, b, out_dtype=BF16, with_stats=True, tm=LN_TM,
                                   name=name)
        return xb, (y, mu, rstd, g, b)


    qkv = _matmul(x, attn_w_in, 0, bm=QKV_BM, bn=MM_BN, out_dtype=BF16, resident=False,
                  name="attn_qkv")
    attn = _attention(qkv, attn_sinks[0], seq=seq, n_heads=n_heads, name="attn_core")
    y = _mm_d(attn, attn_w_out, 0, F32, "attn_out", residual=x)
    xb, res = norm(y, vec(ln_mix_g, 0), vec(ln_mix_b, 0), "attn_ln")
    y = _ffn(xb, ffn_w_gate_up, ffn_w_down, 0, res, name="ffn0")
    xb, res = norm(y, vec(ln_ffn_g, 0), vec(ln_ffn_b, 0), "ffn0_ln")

    gated = _conv_mixer(xb, conv_w_in, conv_w, 0, seq=seq, bm=CONV_IN_BM, bn=UP_BN,
                        name="conv_mixer")
    y = _mm_d(gated, conv_w_out, 0, F32, "conv_out", ln_residual=res)
    xb, res = norm(y, vec(ln_mix_g, 1), vec(ln_mix_b, 1), "conv_ln")
    y = _ffn(xb, ffn_w_gate_up, ffn_w_down, 1, res, name="ffn1")
    (out,) = _layer_norm(y, vec(ln_ffn_g, 1), vec(ln_ffn_b, 1), out_dtype=F32,
                         with_stats=False, tm=LN_TM, name="ffn1_ln")
    return out.reshape(batch, seq, d)
```

```python
import functools
import math

import jax
import jax.numpy as jnp
from jax import lax
from jax.experimental import pallas as pl
from jax.experimental.pallas import tpu as pltpu

DEPTH = 2
HEAD_DIM = 128
GROUP = 4
BLOCK = 128
CONV_WIDTH = 3
ALPHA = (2.0 * DEPTH) ** 0.25
LN_EPS = 1e-5
LOG2_E = math.log2(math.e)

BF16_SUBLANE = 16
VMEM_LIMIT_BYTES = 58 * 1024 * 1024

MM_BM = 2048
MM_BN = 512
UP_BN = 256
CONV_IN_BM = 2048
ROW_CHUNK = 512
QKV_BM = 1024
DOWN_BM = 512
DOWN_BN = 512
LN_TM = 512
LANE = 128

F32 = jnp.float32
BF16 = jnp.bfloat16


def _params(*semantics):
    return pltpu.CompilerParams(dimension_semantics=semantics,
                                vmem_limit_bytes=VMEM_LIMIT_BYTES)


def _resident_rows(bm, k):
    return pl.BlockSpec((bm, k), lambda i, j: (i, 0), pipeline_mode=pl.Buffered(1))


def _w_cols(w, layer, bn, first_block=0):
    k = w.shape[1]
    return pl.BlockSpec((None, k, bn), lambda i, j: (layer, 0, first_block + j))


def _dot_row_chunks(x_ref, w, emit):
    chunk = min(ROW_CHUNK, x_ref.shape[0])
    for r in range(0, x_ref.shape[0], chunk):
        rows = slice(r, r + chunk)
        emit(rows, jnp.dot(x_ref[rows, :].astype(BF16), w, preferred_element_type=F32))


def _mm_kernel(x_ref, w_ref, o_ref):
    def emit(rows, acc):
        o_ref[rows, :] = acc.astype(o_ref.dtype)
    _dot_row_chunks(x_ref, w_ref[...].astype(BF16), emit)


def _mm_residual_kernel(x_ref, w_ref, r_ref, o_ref):
    def emit(rows, acc):
        o_ref[rows, :] = ALPHA * r_ref[rows, :] + acc
    _dot_row_chunks(x_ref, w_ref[...].astype(BF16), emit)


def _mm_ln_residual_kernel(x_ref, w_ref, y_ref, mu_ref, rstd_ref, g_ref, b_ref, o_ref):
    reps = y_ref.shape[1] // LANE
    j = pl.program_id(1)
    g, b = g_ref[j], b_ref[j]

    def emit(rows, acc):
        mu = jnp.tile(mu_ref[rows, :], (1, reps))
        rstd = jnp.tile(rstd_ref[rows, :], (1, reps))
        r = (y_ref[rows, :] - mu) * rstd * g + b
        o_ref[rows, :] = ALPHA * r + acc
    _dot_row_chunks(x_ref, w_ref[...].astype(BF16), emit)


def _matmul(x, w, layer, *, bm, bn, out_dtype, resident, name, residual=None,
            ln_residual=None):
    m, k = x.shape
    if layer is None:
        assert w.shape[1:] == (k, bn)
        n = w.shape[0] * bn
        w_spec = pl.BlockSpec((None, k, bn), lambda i, j: (j, 0, 0))
    else:
        n = w.shape[2]
        w_spec = _w_cols(w, layer, bn)
    x_spec = (_resident_rows(bm, k) if resident
              else pl.BlockSpec((bm, k), lambda i, j: (i, 0)))
    tile = pl.BlockSpec((bm, bn), lambda i, j: (i, j))
    in_specs, args, body = [x_spec, w_spec], [x, w], _mm_kernel
    if residual is not None:
        in_specs, args, body = in_specs + [tile], args + [residual], _mm_residual_kernel
    if ln_residual is not None:
        y, mu, rstd, g, b = ln_residual
        stat = pl.BlockSpec((bm, LANE), lambda i, j: (i, 0))
        vec = pl.BlockSpec((n // bn, 1, bn), lambda i, j: (0, 0, 0))
        in_specs = in_specs + [tile, stat, stat, vec, vec]
        args = args + [y, mu, rstd, g.reshape(vec.block_shape), b.reshape(vec.block_shape)]
        body = _mm_ln_residual_kernel
    return pl.pallas_call(
        body,
        grid=(m // bm, n // bn),
        in_specs=in_specs,
        out_specs=tile,
        out_shape=jax.ShapeDtypeStruct((m, n), out_dtype),
        compiler_params=_params("parallel", "arbitrary"),
        name=name,
    )(*args)


def _swiglu_kernel(x_ref, wg_ref, wu_ref, wd_ref, o_ref, wdb_ref):
    wg = wg_ref[...].astype(BF16)
    wu = wu_ref[...].astype(BF16)
    for r in range(0, x_ref.shape[0], ROW_CHUNK):
        x = x_ref[r:r + ROW_CHUNK, :]
        gate = jnp.dot(x, wg, preferred_element_type=F32)
        up = jnp.dot(x, wu, preferred_element_type=F32)
        o_ref[r:r + ROW_CHUNK, :] = (gate * jax.nn.sigmoid(gate) * up).astype(o_ref.dtype)
    wn = wdb_ref.shape[2]
    for t in range(wdb_ref.shape[0]):
        wdb_ref[t] = wd_ref[:, t * wn:(t + 1) * wn].astype(BF16)


def _swiglu_up(x, w_gate_up, w_down, layer, *, bm, bn, down_bn, name):
    m, k = x.shape
    _, hidden, d_out = w_down.shape
    ni, nj = m // bm, hidden // bn
    slab = hidden // (ni * nj)
    assert slab * ni * nj == hidden and slab % BF16_SUBLANE == 0
    w_tiles = d_out // down_bn
    return pl.pallas_call(
        _swiglu_kernel,
        grid=(ni, nj),
        in_specs=[pl.BlockSpec((bm, k), lambda i, j: (i, 0)),
                  _w_cols(w_gate_up, layer, bn),
                  _w_cols(w_gate_up, layer, bn, first_block=nj),
                  pl.BlockSpec((None, slab, d_out), lambda i, j: (layer, i * nj + j, 0))],
        out_specs=[pl.BlockSpec((bm, bn), lambda i, j: (i, j)),
                   pl.BlockSpec((w_tiles, slab, down_bn), lambda i, j: (0, i * nj + j, 0))],
        out_shape=[jax.ShapeDtypeStruct((m, hidden), BF16),
                   jax.ShapeDtypeStruct((w_tiles, hidden, down_bn), BF16)],
        compiler_params=_params("parallel", "arbitrary"),
        name=name,
    )(x, w_gate_up, w_gate_up, w_down)


def _conv_mixer_kernel(x_ref, wb_ref, wc_ref, wh_ref, cw_ref, o_ref, tail_ref, *,
                       tiles_per_seq):
    i, j = pl.program_id(0), pl.program_id(1)
    halo = tail_ref.shape[1]

    @pl.when(i == 0)
    def _():
        tail_ref[j] = jnp.zeros(tail_ref.shape[1:], F32)

    wb = wb_ref[...].astype(BF16)
    wc = wc_ref[...].astype(BF16)
    wh = wh_ref[...].astype(BF16)
    cw = cw_ref[j]
    w0, w1, w2 = cw[0:1, :], cw[1:2, :], cw[2:3, :]

    def gated(u, u1, u2, b):
        return (b * (u2 * w0 + u1 * w1 + u * w2)).astype(BF16)

    tail = jnp.where((i % tiles_per_seq) != 0, tail_ref[j], 0.0)
    for r in range(0, x_ref.shape[0], ROW_CHUNK):
        x = x_ref[r:r + ROW_CHUNK, :]
        b = jnp.dot(x, wb, preferred_element_type=F32)
        c = jnp.dot(x, wc, preferred_element_type=F32)
        h = jnp.dot(x, wh, preferred_element_type=F32)
        u = c * h
        o_ref[r:r + ROW_CHUNK, :] = gated(u, pltpu.roll(u, 1, axis=0),
                                          pltpu.roll(u, 2, axis=0), b)
        ext = jnp.concatenate([tail, u[0:halo]], axis=0)
        o_ref[r:r + halo, :] = gated(ext[halo:], pltpu.roll(ext, 1, axis=0)[halo:],
                                     pltpu.roll(ext, 2, axis=0)[halo:], b[0:halo])
        tail = u[ROW_CHUNK - halo:]
    tail_ref[j] = tail


def _conv_mixer(x, w_in, conv_w, layer, *, seq, bm, bn, name):
    m, k = x.shape
    ch = w_in.shape[2] // 3
    taps = conv_w.shape[1]
    nj = ch // bn
    assert seq % bm == 0 and bm % ROW_CHUNK == 0
    tap_tiles = conv_w[layer].reshape(taps, nj, bn).transpose(1, 0, 2)
    kernel = functools.partial(_conv_mixer_kernel, tiles_per_seq=seq // bm)
    return pl.pallas_call(
        kernel,
        grid=(m // bm, nj),
        in_specs=[_resident_rows(bm, k),
                  _w_cols(w_in, layer, bn),
                  _w_cols(w_in, layer, bn, first_block=nj),
                  _w_cols(w_in, layer, bn, first_block=2 * nj),
                  pl.BlockSpec((nj, taps, bn), lambda i, j: (0, 0, 0))],
        out_specs=pl.BlockSpec((bm, bn), lambda i, j: (i, j)),
        out_shape=jax.ShapeDtypeStruct((m, ch), BF16),
        scratch_shapes=[pltpu.VMEM((nj, BF16_SUBLANE, bn), F32)],
        compiler_params=_params("arbitrary", "arbitrary"),
        name=name,
    )(x, w_in, w_in, w_in, tap_tiles)


def _ln_kernel(y_ref, g_ref, b_ref, o_ref, *stat_refs):
    y = y_ref[...]
    mu = jnp.mean(y, axis=-1, keepdims=True)
    yc = y - mu
    var = jnp.mean(yc * yc, axis=-1, keepdims=True)
    rstd = lax.rsqrt(var + LN_EPS)
    o_ref[...] = (yc * rstd * g_ref[...] + b_ref[...]).astype(o_ref.dtype)
    if stat_refs:
        mu_ref, rstd_ref = stat_refs
        mu_ref[...] = jnp.broadcast_to(mu, mu_ref.shape)
        rstd_ref[...] = jnp.broadcast_to(rstd, rstd_ref.shape)


def _layer_norm(y, g, b, *, out_dtype, with_stats, tm, name):
    m, d = y.shape
    row = pl.BlockSpec((tm, d), lambda i: (i, 0))
    vec = pl.BlockSpec((1, d), lambda i: (0, 0))
    stat = pl.BlockSpec((tm, LANE), lambda i: (i, 0))
    out_specs = [row] + [stat, stat] * with_stats
    out_shape = ([jax.ShapeDtypeStruct((m, d), out_dtype)]
                 + [jax.ShapeDtypeStruct((m, LANE), F32)] * (2 * with_stats))
    return pl.pallas_call(
        _ln_kernel,
        grid=(m // tm,),
        in_specs=[row, vec, vec],
        out_specs=out_specs,
        out_shape=out_shape,
        compiler_params=_params("parallel"),
        name=name,
    )(y, g, b)


def _attn_kernel(sinks_ref, q_ref, kc_ref, kp_ref, vc_ref, vp_ref, o_ref, *,
                 n_kv_heads, blocks_per_seq):
    i = pl.program_id(0)
    has_prev = (i % blocks_per_seq) != 0
    nq = GROUP * BLOCK
    kk = lax.broadcasted_iota(jnp.int32, (2 * BLOCK, nq), 0)
    qi = lax.broadcasted_iota(jnp.int32, (2 * BLOCK, nq), 1) % BLOCK
    valid_cur = (kk >= BLOCK) & (kk - BLOCK <= qi)
    valid_prev = (kk < BLOCK) & (kk > qi) & has_prev
    bias = jnp.where(valid_cur | valid_prev, 0.0, -jnp.inf).astype(F32)
    lane_group = lax.broadcasted_iota(jnp.int32, (1, nq), 1) // BLOCK
    scale2 = LOG2_E / math.sqrt(HEAD_DIM)

    for h in range(n_kv_heads):
        ks = slice(h * HEAD_DIM, (h + 1) * HEAD_DIM)
        q = jnp.concatenate(
            [q_ref[:, (h * GROUP + g) * HEAD_DIM:(h * GROUP + g + 1) * HEAD_DIM]
             for g in range(GROUP)], axis=0)
        k = jnp.concatenate([kp_ref[:, ks], kc_ref[:, ks]], axis=0)
        v = jnp.concatenate([vp_ref[:, ks], vc_ref[:, ks]], axis=0)
        st = lax.dot_general(k, q, (((1,), (1,)), ((), ())),
                             preferred_element_type=F32) * scale2 + bias
        sink = jnp.zeros((1, nq), F32)
        for g in range(GROUP):
            sink = jnp.where(lane_group == g, sinks_ref[h * GROUP + g] * LOG2_E, sink)
        m = jnp.maximum(jnp.max(st, axis=0, keepdims=True), sink)
        p = jnp.exp2(st - m)
        denom = jnp.sum(p, axis=0, keepdims=True) + jnp.exp2(sink - m)
        probs = (p / denom).astype(BF16)
        o = lax.dot_general(probs, v, (((0,), (0,)), ((), ())),
                            preferred_element_type=F32)
        for g in range(GROUP):
            col = (h * GROUP + g) * HEAD_DIM
            o_ref[:, col:col + HEAD_DIM] = o[g * BLOCK:(g + 1) * BLOCK].astype(BF16)


def _attention(qkv, sinks, *, seq, n_heads, name):
    m = qkv.shape[0]
    n_kv_heads = n_heads // GROUP
    q_width = n_heads * HEAD_DIM
    kv_width = n_kv_heads * HEAD_DIM
    k_col = q_width // kv_width
    v_col = k_col + 1
    prev = lambda i: jnp.maximum(i - 1, 0)
    kernel = functools.partial(_attn_kernel, n_kv_heads=n_kv_heads,
                               blocks_per_seq=seq // BLOCK)
    return pl.pallas_call(
        kernel,
        grid=(m // BLOCK,),
        in_specs=[pl.BlockSpec(memory_space=pltpu.SMEM),
                  pl.BlockSpec((BLOCK, q_width), lambda i: (i, 0)),
                  pl.BlockSpec((BLOCK, kv_width), lambda i: (i, k_col)),
                  pl.BlockSpec((BLOCK, kv_width), lambda i: (prev(i), k_col)),
                  pl.BlockSpec((BLOCK, kv_width), lambda i: (i, v_col)),
                  pl.BlockSpec((BLOCK, kv_width), lambda i: (prev(i), v_col))],
        out_specs=pl.BlockSpec((BLOCK, q_width), lambda i: (i, 0)),
        out_shape=jax.ShapeDtypeStruct((m, q_width), BF16),
        compiler_params=_params("parallel"),
        name=name,
    )(sinks, qkv, qkv, qkv, qkv, qkv)


def _mm_d(x, w, layer, out_dtype, name, **residual):
    return _matmul(x, w, layer, bm=MM_BM, bn=MM_BN, out_dtype=out_dtype,
                   resident=bool(residual), name=name, **residual)


def _ffn(xb, w_gate_up, w_down, layer, ln_residual, *, name):
    act, w_down_tiles = _swiglu_up(xb, w_gate_up, w_down, layer, bm=MM_BM, bn=UP_BN,
                                   down_bn=DOWN_BN, name=name + "_up")
    return _matmul(act, w_down_tiles, None, bm=DOWN_BM, bn=DOWN_BN, out_dtype=F32,
                   resident=False, ln_residual=ln_residual, name=name + "_down")


def kernel(x, attn_w_in, attn_sinks, attn_w_out, conv_w_in, conv_w, conv_w_out,
           ln_mix_g, ln_mix_b, ffn_w_gate_up, ffn_w_down, ln_ffn_g, ln_ffn_b):
    batch, seq, d = x.shape
    m = batch * seq
    n_heads = attn_sinks.shape[1]
    x = x.reshape(m, d)
    vec = lambda p, i: p[i].reshape(1, d)

    def norm(y, g, b, name):
        xb, mu, rstd = _layer_norm(y, g, b, out_dtype=BF16, with_stats=True, tm=LN_TM,
                                   name=name)
        return xb, (y, mu, rstd, g, b)


    qkv = _matmul(x, attn_w_in, 0, bm=QKV_BM, bn=MM_BN, out_dtype=BF16, resident=False,
                  name="attn_qkv")
    attn = _attention(qkv, attn_sinks[0], seq=seq, n_heads=n_heads, name="attn_core")
    y = _mm_d(attn, attn_w_out, 0, F32, "attn_out", residual=x)
    xb, res = norm(y, vec(ln_mix_g, 0), vec(ln_mix_b, 0), "attn_ln")
    y = _ffn(xb, ffn_w_gate_up, ffn_w_down, 0, res, name="ffn0")
    xb, res = norm(y, vec(ln_ffn_g, 0), vec(ln_ffn_b, 0), "ffn0_ln")

    gated = _conv_mixer(xb, conv_w_in, conv_w, 0, seq=seq, bm=CONV_IN_BM, bn=UP_BN,
                        name="conv_mixer")
    y = _mm_d(gated, conv_w_out, 0, F32, "conv_out", ln_residual=res)
    xb, res = norm(y, vec(ln_mix_g, 1), vec(ln_mix_b, 1), "conv_ln")
    y = _ffn(xb, ffn_w_gate_up, ffn_w_down, 1, res, name="ffn1")
    (out,) = _layer_norm(y, vec(ln_ffn_g, 1), vec(ln_ffn_b, 1), out_dtype=F32,
                         with_stats=False, tm=LN_TM, name="ffn1_ln")
    return out.reshape(batch, seq, d)
```
